```python
import math
import jax, jax.numpy as jnp
from jax import lax
import numpy as np

D_MODEL = 2048
BATCH = 4
SEQ = 2048
DEPTH = 1
DEC_BATCH = 128
DEC_SEQ = 4
PAST_LEN = 16384
PAGE_SIZE = 128

N_HEADS_A = 8
HEAD_DK = 128
HEAD_DV = 128
QK_WIDTH = N_HEADS_A * HEAD_DK
V_WIDTH = N_HEADS_A * HEAD_DV
CONV_W = 4
CONV_CH = 2 * QK_WIDTH + V_WIDTH
CHUNK = 64
POOL_WINDOWS = (2, 4, 8, 16)
N_POOL_GROUPS = 4
POOL_WIDTH = D_MODEL // 2
POOL_GROUP = POOL_WIDTH // N_POOL_GROUPS
POOL_MAX = 16
D_FF = ((8 * D_MODEL + 3 * 256 - 1) // (3 * 256)) * 256
IN_SIZES = (QK_WIDTH, QK_WIDTH, V_WIDTH, V_WIDTH, N_HEADS_A, N_HEADS_A, POOL_WIDTH, D_MODEL, D_MODEL)
IN_WIDTH = 2 * QK_WIDTH + 2 * V_WIDTH + 2 * N_HEADS_A + POOL_WIDTH + 2 * D_MODEL
EPS = 1e-6

kernel_name = 'gated_delta_pool_hybrid_step'


def rms_norm(x, gain):
    xf = x.astype(jnp.float32)
    y = xf * lax.rsqrt(jnp.mean(xf * xf, axis=-1, keepdims=True) + EPS)
    return (y * gain.astype(jnp.float32)).astype(x.dtype)


def l2_norm(x):
    xf = x.astype(jnp.float32)
    return xf * lax.rsqrt(jnp.sum(xf * xf, axis=-1, keepdims=True) + EPS)


def ada_modulate(h, shift, scale):
    return h * (1.0 + scale[:, None, :]) + shift[:, None, :]


def causal_conv_silu(x, buf, w):
    T = x.shape[1]
    xp = jnp.concatenate([buf.astype(x.dtype), x], axis=1)
    y = xp[:, 0:T] * w[0]
    for j in range(1, CONV_W):
        y = y + xp[:, j:j + T] * w[j]
    return jax.nn.silu(y), xp[:, -(CONV_W - 1):]


def gated_delta_chunked(q, k, v, g, beta, s0):
    B, T, H, DK = q.shape
    DV = v.shape[-1]
    C = CHUNK if T % CHUNK == 0 else T
    N = T // C

    def blk(t):
        t = t.reshape((B, N, C) + t.shape[2:])
        return jnp.moveaxis(t, 3, 2) if t.ndim == 5 else jnp.moveaxis(t, 3, 2)

    q = blk(q) * (DK ** -0.5)
    k = blk(k)
    v = blk(v)
    g = blk(g)
    beta = blk(beta)
    decay = jnp.cumsum(g, axis=-1)
    idx = jnp.arange(C)
    causal = idx[:, None] >= idx[None, :]
    strict = idx[:, None] > idx[None, :]
    diff = decay[..., :, None] - decay[..., None, :]
    lmask = jnp.where(causal, jnp.exp(jnp.where(causal, diff, 0.0)), 0.0)
    kb = k * beta[..., None]
    m = jnp.where(strict, jnp.einsum('bnhrd,bnhsd->bnhrs', kb, k) * lmask, 0.0)
    eye = jnp.eye(C, dtype=jnp.float32)
    tmat = lax.linalg.triangular_solve(eye + m, jnp.broadcast_to(eye, m.shape), left_side=True, lower=True, unit_diagonal=True)
    u_base = jnp.einsum('bnhrs,bnhsv->bnhrv', tmat, v * beta[..., None])
    w_dec = jnp.einsum('bnhrs,bnhsd->bnhrd', tmat, kb * jnp.exp(decay)[..., None])
    qk = jnp.einsum('bnhrd,bnhsd->bnhrs', q, k) * lmask
    last = decay[..., -1:]
    k_tail = k * jnp.exp(last - decay)[..., None]
    q_dec = q * jnp.exp(decay)[..., None]
    chunk_decay = jnp.exp(last[..., 0])
    xs = (u_base.swapaxes(0, 1), w_dec.swapaxes(0, 1), qk.swapaxes(0, 1), k_tail.swapaxes(0, 1), q_dec.swapaxes(0, 1), chunk_decay.swapaxes(0, 1))

    def step(s, inp):
        u_b, w_d, qk_c, k_t, q_d, cd = inp
        u = u_b - jnp.einsum('bhrd,bhdv->bhrv', w_d, s)
        o = jnp.einsum('bhrd,bhdv->bhrv', q_d, s) + jnp.einsum('bhrs,bhsv->bhrv', qk_c, u)
        s = s * cd[..., None, None] + jnp.einsum('bhrd,bhrv->bhdv', k_t, u)
        return s, o

    s_fin, o = lax.scan(step, s0, xs)
    o = jnp.transpose(o, (1, 0, 3, 2, 4)).reshape(B, T, H, DV)
    return o, s_fin


def multiscale_pool(x, buf, pos0):
    B, T, W = x.shape
    P = POOL_MAX - 1
    xp = jnp.concatenate([buf.astype(x.dtype), x], axis=1)
    cs = jnp.cumsum(xp.astype(jnp.float32), axis=1)
    cs = jnp.concatenate([jnp.zeros((B, 1, W), jnp.float32), cs], axis=1)
    pos = pos0 + jnp.arange(T)
    xf = x.astype(jnp.float32)
    outs = []
    for gi in range(N_POOL_GROUPS):
        w = POOL_WINDOWS[gi]
        lo, hi = gi * POOL_GROUP, (gi + 1) * POOL_GROUP
        s = cs[:, P + 1:P + 1 + T, lo:hi] - cs[:, P + 1 - w:P + 1 - w + T, lo:hi]
        cnt = jnp.minimum(w, pos + 1).astype(jnp.float32)
        outs.append(s / cnt[None, :, None] - xf[:, :, lo:hi])
    return jnp.concatenate(outs, axis=-1).astype(x.dtype), xp[:, -P:]


def token_mixers(u, conv_buf, s0, pool_buf, pos0, w_in, conv_w, a_log, dt_bias, o_norm_g, pool_w, pool_scale, w_proj_a, w_proj_b, w_out):
    B, T, _ = u.shape
    f32 = jnp.float32
    proj = u @ w_in
    splits = np.cumsum(IN_SIZES)[:-1].tolist()
    q, k, v, z, a, b, xpool, ga, gb = jnp.split(proj, splits, axis=-1)
    qkv, conv_new = causal_conv_silu(jnp.concatenate([q, k, v], axis=-1), conv_buf, conv_w)
    q, k, v = jnp.split(qkv, [QK_WIDTH, 2 * QK_WIDTH], axis=-1)
    q = l2_norm(q.reshape(B, T, N_HEADS_A, HEAD_DK))
    k = l2_norm(k.reshape(B, T, N_HEADS_A, HEAD_DK))
    v = v.reshape(B, T, N_HEADS_A, HEAD_DV).astype(f32)
    beta = jax.nn.sigmoid(b.astype(f32))
    g = -jnp.exp(a_log.astype(f32)) * jax.nn.softplus(a.astype(f32) + dt_bias.astype(f32))
    o, s_new = gated_delta_chunked(q, k, v, g, beta, s0.astype(f32))
    o = rms_norm(o, o_norm_g) * jax.nn.silu(z.reshape(B, T, N_HEADS_A, HEAD_DV).astype(f32))
    y_a = o.reshape(B, T, V_WIDTH).astype(u.dtype) @ w_proj_a
    y_p, pool_new = multiscale_pool(xpool, pool_buf, pos0)
    y_p = jnp.einsum('btgc,gcd->btgd', y_p.reshape(B, T, N_POOL_GROUPS, POOL_GROUP), pool_w).reshape(B, T, POOL_WIDTH) * pool_scale
    y_b = y_p @ w_proj_b
    merged = jax.nn.sigmoid(ga) * y_a + jax.nn.sigmoid(gb) * y_b
    return merged @ w_out, s_new.astype(s0.dtype), conv_new, pool_new


def decoder_layer(x, c, s0, conv_buf, pool_buf, pos0, w_ada, b_ada, norm1_g, w_in, conv_w, a_log, dt_bias, o_norm_g, pool_w, pool_scale, w_proj_a, w_proj_b, w_out, norm2_g, w_gate_up, w_down):
    mod = jax.nn.silu(c) @ w_ada + b_ada
    sh1, sc1, gt1, sh2, sc2, gt2 = jnp.split(mod, 6, axis=-1)
    u = ada_modulate(rms_norm(x, norm1_g), sh1, sc1)
    mix, s_new, conv_new, pool_new = token_mixers(u, conv_buf, s0, pool_buf, pos0, w_in, conv_w, a_log, dt_bias, o_norm_g, pool_w, pool_scale, w_proj_a, w_proj_b, w_out)
    x = x + gt1[:, None, :] * mix
    h = ada_modulate(rms_norm(x, norm2_g), sh2, sc2)
    gate, up = jnp.split(h @ w_gate_up, 2, axis=-1)
    x = x + gt2[:, None, :] * ((jax.nn.silu(gate) * up) @ w_down)
    return x, s_new, conv_new, pool_new


def setup_inputs(seed: int = 0) -> dict:
    key = jax.random.key(seed)
    ks = jax.random.split(key, 26)
    f32 = jnp.float32
    L = DEPTH

    def nrm(k, shape, s):
        return jax.random.normal(k, shape, f32) * s

    dt = jnp.exp(jax.random.uniform(ks[13], (L, N_HEADS_A), f32, math.log(1e-3), math.log(1e-1)))
    dt_bias = dt + jnp.log(-jnp.expm1(-dt))
    return {
        'x_prompt': nrm(ks[0], (BATCH, SEQ, D_MODEL), 1.0),
        'x_sample': nrm(ks[1], (DEC_BATCH, DEC_SEQ, D_MODEL), 1.0),
        'c_prompt': nrm(ks[2], (BATCH, D_MODEL), 1.0),
        'c_sample': nrm(ks[3], (DEC_BATCH, D_MODEL), 1.0),
        'state_delta': nrm(ks[4], (L, DEC_BATCH, N_HEADS_A, HEAD_DK, HEAD_DV), 0.1),
        'state_conv': nrm(ks[5], (L, DEC_BATCH, CONV_W - 1, CONV_CH), 1.0),
        'state_pool': nrm(ks[6], (L, DEC_BATCH, POOL_MAX - 1, POOL_WIDTH), 1.0),
        'w_ada': nrm(ks[7], (L, D_MODEL, 6 * D_MODEL), 0.5 * D_MODEL ** -0.5),
        'b_ada': nrm(ks[8], (L, 6 * D_MODEL), 0.01),
        'norm1_g': 1.0 + nrm(ks[9], (L, D_MODEL), 0.02),
        'w_in': nrm(ks[10], (L, D_MODEL, IN_WIDTH), D_MODEL ** -0.5),
        'conv_w': nrm(ks[11], (L, CONV_W, CONV_CH), CONV_W ** -0.5),
        'a_log': jnp.log(jax.random.uniform(ks[12], (L, N_HEADS_A), f32, 1.0, 16.0)),
        'dt_bias': dt_bias,
        'o_norm_g': 1.0 + nrm(ks[14], (L, HEAD_DV), 0.02),
        'pool_w': nrm(ks[15], (L, N_POOL_GROUPS, POOL_GROUP, POOL_GROUP), POOL_GROUP ** -0.5),
        'pool_scale': 1.0 + nrm(ks[16], (L, POOL_WIDTH), 0.1),
        'w_proj_a': nrm(ks[17], (L, V_WIDTH, D_MODEL), V_WIDTH ** -0.5),
        'w_proj_b': nrm(ks[18], (L, POOL_WIDTH, D_MODEL), POOL_WIDTH ** -0.5),
        'w_out': nrm(ks[19], (L, D_MODEL, D_MODEL), D_MODEL ** -0.5),
        'norm2_g': 1.0 + nrm(ks[20], (L, D_MODEL), 0.02),
        'w_gate_up': nrm(ks[21], (L, D_MODEL, 2 * D_FF), D_MODEL ** -0.5),
        'w_down': nrm(ks[22], (L, D_FF, D_MODEL), D_FF ** -0.5),
        'final_g': 1.0 + nrm(ks[23], (D_MODEL,), 0.02),
    }


def reference(x_prompt, x_sample, c_prompt, c_sample, state_delta, state_conv, state_pool, w_ada, b_ada, norm1_g, w_in, conv_w, a_log, dt_bias, o_norm_g, pool_w, pool_scale, w_proj_a, w_proj_b, w_out, norm2_g, w_gate_up, w_down, final_g):
    xp, xs = x_prompt, x_sample
    dp, cp, pp, ds, cs, ps = [], [], [], [], [], []
    for l in range(DEPTH):
        lw = (w_ada[l], b_ada[l], norm1_g[l], w_in[l], conv_w[l], a_log[l], dt_bias[l], o_norm_g[l], pool_w[l], pool_scale[l], w_proj_a[l], w_proj_b[l], w_out[l], norm2_g[l], w_gate_up[l], w_down[l])
        s0_p = jnp.zeros((BATCH, N_HEADS_A, HEAD_DK, HEAD_DV), state_delta.dtype)
        conv0_p = jnp.zeros((BATCH, CONV_W - 1, CONV_CH), xp.dtype)
        pool0_p = jnp.zeros((BATCH, POOL_MAX - 1, POOL_WIDTH), xp.dtype)
        xp, s_p, c_p, p_p = decoder_layer(xp, c_prompt, s0_p, conv0_p, pool0_p, 0, *lw)
        xs, s_s, c_s, p_s = decoder_layer(xs, c_sample, state_delta[l], state_conv[l], state_pool[l], PAST_LEN, *lw)
        dp.append(s_p); cp.append(c_p); pp.append(p_p)
        ds.append(s_s); cs.append(c_s); ps.append(p_s)
    y_prompt = rms_norm(xp, final_g)
    y_sample = rms_norm(xs, final_g)
    return (y_prompt, y_sample, jnp.stack(dp), jnp.stack(cp), jnp.stack(pp), jnp.stack(ds), jnp.stack(cs), jnp.stack(ps))
```

```python
import functools

import jax
import jax.numpy as jnp
from jax import lax
from jax.experimental import pallas as pl
from jax.experimental.pallas import tpu as pltpu

F32 = jnp.float32
BF16 = jnp.bfloat16

D = 2048
NH = 8
DH = 128
QKW = NH * DH
CONVC = 3 * QKW
CONVW = 4
PW = 1024
PGRP = 256
POOL_WINDOWS = (2, 4, 8, 16)
PHIST = 15
DFF = 5632
EPS = 1e-6

BP, TP = 4, 2048
BS, TS = 128, 4
RP = BP * TP
RS = BS * TS
R = RP + RS
TM = 512
NPT = RP // TM
TPB = TP // TM
CHUNK = 64
SPAD = 8
SSEQ = 16

VMEM_LIMIT = 56 * 1024 * 1024

NT_DIMS = (((1,), (1,)), ((), ()))
TN_DIMS = (((0,), (0,)), ((), ()))


def _params(*sem):
    return pltpu.CompilerParams(dimension_semantics=sem, vmem_limit_bytes=VMEM_LIMIT)


def _sigmoid(x):
    return 1.0 / (1.0 + jnp.exp(-x))


def _silu(x):
    return x * _sigmoid(x)


def _softplus(x):
    return jnp.maximum(x, 0.0) + jnp.log1p(jnp.exp(-jnp.abs(x)))


def _rms(x, gain):
    return x * lax.rsqrt(jnp.mean(x * x, axis=-1, keepdims=True) + EPS) * gain


def _dot(a, b):
    return jnp.dot(a, b, preferred_element_type=F32)


def _mod_kernel(c_ref, w_ref, b_ref, o_ref):
    c = c_ref[...]
    o_ref[...] = _dot(_silu(c).astype(BF16), w_ref[...]) + b_ref[...]


def _mod(c_all, w_ada, b_ada):
    n = c_all.shape[0]
    tn = 1024
    return pl.pallas_call(
        _mod_kernel,
        grid=(6 * D // tn,),
        in_specs=[pl.BlockSpec((n, D), lambda j: (0, 0)),
                  pl.BlockSpec((D, tn), lambda j: (0, j)),
                  pl.BlockSpec((1, tn), lambda j: (0, j))],
        out_specs=pl.BlockSpec((n, tn), lambda j: (0, j)),
        out_shape=jax.ShapeDtypeStruct((n, 6 * D), F32),
        compiler_params=_params("arbitrary"),
        name="ada_mod",
    )(c_all, w_ada, b_ada)


def _mod_specs(col, tiles_per_seq, n_prompt_tiles):
    p = pl.BlockSpec((1, 1, D), lambda i, *_: (jnp.minimum(i // tiles_per_seq, BP - 1), 0, col))
    s = pl.BlockSpec((BS, D), lambda i, *_: (0, col))
    return p, s


def _pre_kernel(xp_ref, xs_ref, shp_ref, scp_ref, shs_ref, scs_ref, g_ref, u_ref):
    i = pl.program_id(0)
    g = g_ref[...]

    @pl.when(i < NPT)
    def _():
        u = _rms(xp_ref[...], g) * (1.0 + scp_ref[0]) + shp_ref[0]
        u_ref[...] = u.astype(BF16)

    @pl.when(i == NPT)
    def _():
        for t in range(TS):
            rows = slice(t * BS, (t + 1) * BS)
            u = _rms(xs_ref[rows, :], g) * (1.0 + scs_ref[...]) + shs_ref[...]
            u_ref[rows, :] = u.astype(BF16)


def _pre(xp2, xs2, mod_p, mod_s, g):
    shp, shs = _mod_specs(0, TPB, NPT)
    scp, scs = _mod_specs(1, TPB, NPT)
    return pl.pallas_call(
        _pre_kernel,
        grid=(NPT + 1,),
        in_specs=[pl.BlockSpec((TM, D), lambda i: (jnp.minimum(i, NPT - 1), 0)),
                  pl.BlockSpec((TM, D), lambda i: (0, 0)),
                  shp, scp, shs, scs,
                  pl.BlockSpec((1, D), lambda i: (0, 0))],
        out_specs=pl.BlockSpec((TM, D), lambda i: (i, 0)),
        out_shape=jax.ShapeDtypeStruct((R, D), BF16),
        compiler_params=_params("arbitrary"),
        name="norm1_mod",
    )(xp2, xs2, mod_p, mod_p, mod_s, mod_s, g)


def _mm_kernel(a_ref, w_ref, o_ref):
    o_ref[...] = _dot(a_ref[...], w_ref[...]).astype(o_ref.dtype)


def _matmul(a, w, n_cols, tn, name):
    rows, k = a.shape
    return pl.pallas_call(
        _mm_kernel,
        grid=(n_cols // tn, rows // TM),
        in_specs=[pl.BlockSpec((TM, k), lambda j, i: (i, 0)),
                  pl.BlockSpec((k, tn), lambda j, i: (0, j))],
        out_specs=pl.BlockSpec((TM, tn), lambda j, i: (i, j)),
        out_shape=jax.ShapeDtypeStruct((rows, n_cols), F32),
        compiler_params=_params("arbitrary", "arbitrary"),
        name=name,
    )(a, w)


def _delta_prep(get_qkv, ab, alog, dtb, valid, chunk, seq_block, m_ref, qk_ref, vb_ref, kbd_ref, qd_ref, kt_ref, cd_ref):
    rows = ab.shape[0]
    a = ab[:, :128]
    b = ab[:, 128:]
    beta = _sigmoid(b)
    g = -jnp.exp(alog) * _softplus(a + dtb)
    if valid is not None:
        beta = jnp.where(valid, beta, 0.0)
        g = jnp.where(valid, g, 0.0)
    shift = seq_block.bit_length() - 1
    r = lax.broadcasted_iota(jnp.int32, (rows, rows), 0)
    c = lax.broadcasted_iota(jnp.int32, (rows, rows), 1)
    same = (r >> shift) == (c >> shift)
    ltri = jnp.where(same & (r >= c), 1.0, 0.0)
    lall = jnp.where(same, 1.0, 0.0)
    dcol = jnp.dot(ltri, g, preferred_element_type=F32, precision=lax.Precision.HIGHEST)
    last = jnp.dot(lall, g, preferred_element_type=F32, precision=lax.Precision.HIGHEST)
    cd_ref[...] = jnp.exp(last)
    drow = dcol.T
    edc = jnp.exp(dcol)
    etl = jnp.exp(last - dcol)
    rc = lax.broadcasted_iota(jnp.int32, (chunk, chunk), 0)
    cc = lax.broadcasted_iota(jnp.int32, (chunk, chunk), 1)
    same_c = (rc >> shift) == (cc >> shift)
    causal = same_c & (rc >= cc)
    strict = same_c & (rc > cc)
    for h in range(NH):
        q, k, v = get_qkv(h)
        qn = q * lax.rsqrt(jnp.sum(q * q, axis=-1, keepdims=True) + EPS) * (DH ** -0.5)
        kn = k * lax.rsqrt(jnp.sum(k * k, axis=-1, keepdims=True) + EPS)
        bcol = beta[:, h:h + 1]
        kb = kn * bcol
        cols = slice(h * DH, (h + 1) * DH)
        vb_ref[:, cols] = (v * bcol).astype(BF16)
        kbd_ref[:, cols] = (kb * edc[:, h:h + 1]).astype(BF16)
        qd_ref[:, cols] = (qn * edc[:, h:h + 1]).astype(BF16)
        kt_ref[:, cols] = (kn * etl[:, h:h + 1]).astype(BF16)
        for sc in range(rows // chunk):
            rs = slice(sc * chunk, (sc + 1) * chunk)
            diff = dcol[rs, h:h + 1] - drow[h:h + 1, rs]
            lm = jnp.where(causal, jnp.exp(jnp.where(causal, diff, 0.0)), 0.0)
            lhs = jnp.concatenate([kb[rs], qn[rs]], axis=0).astype(BF16)
            s = lax.dot_general(lhs, kn[rs].astype(BF16), NT_DIMS, preferred_element_type=F32)
            idx = sc * NH + h
            m_ref[idx] = jnp.where(strict, s[:chunk] * lm, 0.0)
            qk_ref[idx] = (s[chunk:] * lm).astype(BF16)


def _prep_prompt_kernel(cur_ref, prev_ref, ab_ref, cw_ref, alog_ref, dtb_ref,
                        m_ref, qk_ref, vb_ref, kbd_ref, qd_ref, kt_ref, cd_ref, xc_ref):
    blk = pl.program_id(1)
    xc_ref[0:8, :] = jnp.where(blk > 0, prev_ref[...], 0.0)
    xc_ref[8:8 + TM, :] = cur_ref[...]
    w = cw_ref[...]

    def conv(cols):
        y = xc_ref[8:8 + TM, cols] * w[CONVW - 1:CONVW, cols]
        for j in range(CONVW - 1):
            y = y + xc_ref[5 + j:5 + j + TM, cols] * w[j:j + 1, cols]
        return _silu(y)

    def get_qkv(h):
        return tuple(conv(slice(p * QKW + h * DH, p * QKW + (h + 1) * DH)) for p in range(3))

    _delta_prep(get_qkv, ab_ref[...], alog_ref[...], dtb_ref[...], None, CHUNK, CHUNK,
                m_ref, qk_ref, vb_ref, kbd_ref, qd_ref, kt_ref, cd_ref)


def _prep_prompt(proj1, proj_ab, conv_w, alog, dtb):
    cpb = TM // CHUNK * NH
    row = lambda b, k: b * TPB + k
    tok = pl.BlockSpec((TM, QKW), lambda b, k: (row(b, k), 0))
    return pl.pallas_call(
        _prep_prompt_kernel,
        grid=(BP, TPB),
        in_specs=[pl.BlockSpec((TM, CONVC), lambda b, k: (row(b, k), 0)),
                  pl.BlockSpec((8, CONVC), lambda b, k: (jnp.maximum(row(b, k) * (TM // 8) - 1, 0), 0)),
                  pl.BlockSpec((TM, 256), lambda b, k: (row(b, k), 0)),
                  pl.BlockSpec((CONVW, CONVC), lambda b, k: (0, 0)),
                  pl.BlockSpec((1, 128), lambda b, k: (0, 0)),
                  pl.BlockSpec((1, 128), lambda b, k: (0, 0))],
        out_specs=[pl.BlockSpec((cpb, CHUNK, CHUNK), lambda b, k: (row(b, k), 0, 0)),
                   pl.BlockSpec((cpb, CHUNK, CHUNK), lambda b, k: (row(b, k), 0, 0)),
                   tok, tok, tok, tok,
                   pl.BlockSpec((TM, 128), lambda b, k: (row(b, k), 0))],
        out_shape=[jax.ShapeDtypeStruct((RP // CHUNK * NH, CHUNK, CHUNK), F32),
                   jax.ShapeDtypeStruct((RP // CHUNK * NH, CHUNK, CHUNK), BF16),
                   jax.ShapeDtypeStruct((RP, QKW), BF16),
                   jax.ShapeDtypeStruct((RP, QKW), BF16),
                   jax.ShapeDtypeStruct((RP, QKW), BF16),
                   jax.ShapeDtypeStruct((RP, QKW), BF16),
                   jax.ShapeDtypeStruct((RP, 128), F32)],
        scratch_shapes=[pltpu.VMEM((TM + 8, CONVC), F32)],
        compiler_params=_params("arbitrary", "arbitrary"),
        name="delta_prep_prompt",
    )(proj1, proj1, proj_ab, conv_w, alog, dtb)


def _prep_sample_kernel(xp_ref, ab_ref, cw_ref, alog_ref, dtb_ref,
                        m_ref, qk_ref, vb_ref, kbd_ref, qd_ref, kt_ref, cd_ref):
    nrow = SSEQ * SPAD
    w = cw_ref[...]
    valid = (lax.broadcasted_iota(jnp.int32, (nrow, 1), 0) & (SPAD - 1)) < TS

    def conv(cols):
        x = xp_ref[:, cols]
        y = x * w[0:1, cols]
        for j in range(1, CONVW):
            y = y + pltpu.roll(x, nrow - j, axis=0) * w[j:j + 1, cols]
        return jnp.where(valid, _silu(y), 0.0)

    def get_qkv(h):
        return tuple(conv(slice(p * QKW + h * DH, p * QKW + (h + 1) * DH)) for p in range(3))

    _delta_prep(get_qkv, ab_ref[...], alog_ref[...], dtb_ref[...], valid, nrow, SPAD,
                m_ref, qk_ref, vb_ref, kbd_ref, qd_ref, kt_ref, cd_ref)


def _prep_sample(xp8, ab8, conv_w, alog, dtb):
    nrow = SSEQ * SPAD
    steps = BS // SSEQ
    tok = pl.BlockSpec((nrow, QKW), lambda s: (s, 0))
    mat = pl.BlockSpec((NH, nrow, nrow), lambda s: (s, 0, 0))
    return pl.pallas_call(
        _prep_sample_kernel,
        grid=(steps,),
        in_specs=[pl.BlockSpec((nrow, CONVC), lambda s: (s, 0)),
                  pl.BlockSpec((nrow, 256), lambda s: (s, 0)),
                  pl.BlockSpec((CONVW, CONVC), lambda s: (0, 0)),
                  pl.BlockSpec((1, 128), lambda s: (0, 0)),
                  pl.BlockSpec((1, 128), lambda s: (0, 0))],
        out_specs=[mat, mat, tok, tok, tok, tok, pl.BlockSpec((nrow, 128), lambda s: (s, 0))],
        out_shape=[jax.ShapeDtypeStruct((steps * NH, nrow, nrow), F32),
                   jax.ShapeDtypeStruct((steps * NH, nrow, nrow), BF16),
                   jax.ShapeDtypeStruct((BS * SPAD, QKW), BF16),
                   jax.ShapeDtypeStruct((BS * SPAD, QKW), BF16),
                   jax.ShapeDtypeStruct((BS * SPAD, QKW), BF16),
                   jax.ShapeDtypeStruct((BS * SPAD, QKW), BF16),
                   jax.ShapeDtypeStruct((BS * SPAD, 128), F32)],
        compiler_params=_params("arbitrary"),
        name="delta_prep_sample",
    )(xp8, ab8, conv_w, alog, dtb)


def _solve_kernel(m_ref, x_ref):
    n, _, lanes = m_ref.shape
    rowid = lax.broadcasted_iota(jnp.int32, (n, lanes), 0)

    def body_i(i, carry):
        def body_j(j, acc):
            return acc - m_ref[i, pl.ds(j, 1), :] * x_ref[j]

        x_ref[i] = lax.fori_loop(0, i, body_j, (rowid == i).astype(F32))
        return carry

    lax.fori_loop(0, n, body_i, 0)


def _solve(m, name):
    n, _, g = m.shape
    lanes = 128
    spec = pl.BlockSpec((n, n, lanes), lambda s: (0, 0, s))
    return pl.pallas_call(
        _solve_kernel,
        grid=(g // lanes,),
        in_specs=[spec],
        out_specs=spec,
        out_shape=jax.ShapeDtypeStruct((n, n, g), F32),
        compiler_params=_params("arbitrary"),
        name=name,
    )(m)


def _scan_prompt_kernel(t_ref, qk_ref, vb_ref, kbd_ref, qd_ref, kt_ref, cd_ref, o_ref, sout_ref, s_ref):
    blk = pl.program_id(1)

    @pl.when(blk == 0)
    def _():
        s_ref[...] = jnp.zeros_like(s_ref)

    for c in range(TM // CHUNK):
        rows = slice(c * CHUNK, (c + 1) * CHUNK)
        cdrow = cd_ref[c * CHUNK:c * CHUNK + 1, :]
        for h in range(NH):
            cols = slice(h * DH, (h + 1) * DH)
            idx = c * NH + h
            rhs = jnp.concatenate([vb_ref[rows, cols], kbd_ref[rows, cols]], axis=1)
            r1 = _dot(t_ref[idx], rhs)
            s = s_ref[h]
            lhs = jnp.concatenate([r1[:, DH:].astype(BF16), qd_ref[rows, cols]], axis=0)
            r2 = _dot(lhs, s.astype(BF16))
            ub = (r1[:, :DH] - r2[:CHUNK]).astype(BF16)
            o_ref[rows, cols] = r2[CHUNK:] + _dot(qk_ref[idx], ub)
            s_ref[h] = s * cdrow[:, h:h + 1] + lax.dot_general(kt_ref[rows, cols], ub, TN_DIMS,
                                                              preferred_element_type=F32)

    @pl.when(blk == TPB - 1)
    def _():
        sout_ref[0] = s_ref[...]


def _scan_prompt(t, qk, vb, kbd, qd, kt, cd):
    cpb = TM // CHUNK * NH
    row = lambda b, k: b * TPB + k
    mat = pl.BlockSpec((cpb, CHUNK, CHUNK), lambda b, k: (row(b, k), 0, 0))
    tok = pl.BlockSpec((TM, QKW), lambda b, k: (row(b, k), 0))
    return pl.pallas_call(
        _scan_prompt_kernel,
        grid=(BP, TPB),
        in_specs=[mat, mat, tok, tok, tok, tok, pl.BlockSpec((TM, 128), lambda b, k: (row(b, k), 0))],
        out_specs=[tok, pl.BlockSpec((1, NH, DH, DH), lambda b, k: (b, 0, 0, 0))],
        out_shape=[jax.ShapeDtypeStruct((RP, QKW), F32),
                   jax.ShapeDtypeStruct((BP, NH, DH, DH), F32)],
        scratch_shapes=[pltpu.VMEM((NH, DH, DH), F32)],
        compiler_params=_params("arbitrary", "arbitrary"),
        name="delta_scan_prompt",
    )(t, qk, vb, kbd, qd, kt, cd)


def _scan_sample_kernel(t_ref, qk_ref, vb_ref, kbd_ref, qd_ref, kt_ref, cd_ref, s0_ref, o_ref, sout_ref):
    nrow = SSEQ * SPAD
    seq_of_row = lax.broadcasted_iota(jnp.int32, (nrow, 1), 0) >> (SPAD.bit_length() - 1)
    for h in range(NH):
        cols = slice(h * DH, (h + 1) * DH)
        rhs = jnp.concatenate([vb_ref[:, cols], kbd_ref[:, cols]], axis=1)
        r1 = _dot(t_ref[h], rhs)
        qd = qd_ref[:, cols].astype(F32)
        u_parts, qs_parts = [], []
        for b in range(SSEQ):
            rows = slice(b * SPAD, (b + 1) * SPAD)
            lhs = jnp.concatenate([r1[rows, DH:], qd[rows]], axis=0).astype(BF16)
            r2 = _dot(lhs, s0_ref[b, h].astype(BF16))
            u_parts.append(r1[rows, :DH] - r2[:SPAD])
            qs_parts.append(r2[SPAD:])
        u = jnp.concatenate(u_parts, axis=0)
        o_ref[:, cols] = jnp.concatenate(qs_parts, axis=0) + _dot(qk_ref[h], u.astype(BF16))
        kt = kt_ref[:, cols]
        for b in range(SSEQ):
            ub = jnp.where(seq_of_row == b, u, 0.0).astype(BF16)
            cd = cd_ref[b * SPAD:b * SPAD + 1, h:h + 1]
            sout_ref[b, h] = s0_ref[b, h] * cd + lax.dot_general(kt, ub, TN_DIMS, preferred_element_type=F32)


def _scan_sample(t, qk, vb, kbd, qd, kt, cd, s0):
    nrow = SSEQ * SPAD
    steps = BS // SSEQ
    mat = pl.BlockSpec((NH, nrow, nrow), lambda s: (s, 0, 0))
    tok = pl.BlockSpec((nrow, QKW), lambda s: (s, 0))
    st = pl.BlockSpec((SSEQ, NH, DH, DH), lambda s: (s, 0, 0, 0))
    return pl.pallas_call(
        _scan_sample_kernel,
        grid=(steps,),
        in_specs=[mat, mat, tok, tok, tok, tok, pl.BlockSpec((nrow, 128), lambda s: (s, 0)), st],
        out_specs=[tok, st],
        out_shape=[jax.ShapeDtypeStruct((BS * SPAD, QKW), F32),
                   jax.ShapeDtypeStruct((BS, NH, DH, DH), F32)],
        compiler_params=_params("arbitrary"),
        name="delta_scan_sample",
    )(t, qk, vb, kbd, qd, kt, cd, s0)


def _post_a_kernel(op_ref, os_ref, z_ref, g_ref, w_ref, y_ref):
    i = pl.program_id(0)
    o = jnp.where(i < NPT, op_ref[...], os_ref[...])
    g = g_ref[...]
    parts = []
    for h in range(NH):
        cols = slice(h * DH, (h + 1) * DH)
        parts.append((_rms(o[:, cols], g) * _silu(z_ref[:, cols])).astype(BF16))
    y_ref[...] = _dot(jnp.concatenate(parts, axis=1), w_ref[...])


def _post_a(o_p, o_s, proj1, o_norm_g, w_proj_a):
    return pl.pallas_call(
        _post_a_kernel,
        grid=(NPT + 1,),
        in_specs=[pl.BlockSpec((TM, QKW), lambda i: (jnp.minimum(i, NPT - 1), 0)),
                  pl.BlockSpec((TM, QKW), lambda i: (0, 0)),
                  pl.BlockSpec((TM, QKW), lambda i: (i, 3)),
                  pl.BlockSpec((1, DH), lambda i: (0, 0)),
                  pl.BlockSpec((QKW, D), lambda i: (0, 0))],
        out_specs=pl.BlockSpec((TM, D), lambda i: (i, 0)),
        out_shape=jax.ShapeDtypeStruct((R, D), F32),
        compiler_params=_params("arbitrary"),
        name="branch_a_out",
    )(o_p, o_s, proj1, o_norm_g, w_proj_a)


def _post_b_kernel(x_ref, prev_ref, hist_ref, pw_ref, ps_ref, w_ref, y_ref, xc_ref, yp_ref):
    i = pl.program_id(0)
    hist_rows = 16

    @pl.when(i < NPT)
    def _():
        k = i % TPB
        xc_ref[0:hist_rows, :] = jnp.where(k > 0, prev_ref[...], 0.0)
        xc_ref[hist_rows:hist_rows + TM, :] = x_ref[...]
        pos = k * TM + lax.broadcasted_iota(jnp.int32, (TM, 1), 0)
        for gi, win in enumerate(POOL_WINDOWS):
            cols = slice(gi * PGRP, (gi + 1) * PGRP)
            acc = x_ref[:, cols]
            for d in range(1, win):
                acc = acc + xc_ref[hist_rows - d:hist_rows - d + TM, cols]
            cnt = jnp.minimum(win, pos + 1).astype(F32)
            yp_ref[:, cols] = acc / cnt - x_ref[:, cols]

    @pl.when(i == NPT)
    def _():
        for gi, win in enumerate(POOL_WINDOWS):
            cols = slice(gi * PGRP, (gi + 1) * PGRP)
            for t in range(TS):
                rows = slice(t * BS, (t + 1) * BS)
                acc = x_ref[rows, cols]
                for d in range(1, win):
                    src = PHIST + t - d
                    if src >= PHIST:
                        acc = acc + x_ref[(src - PHIST) * BS:(src - PHIST + 1) * BS, cols]
                    else:
                        acc = acc + hist_ref[src * BS:(src + 1) * BS, cols]
                yp_ref[rows, cols] = acc / float(win) - x_ref[rows, cols]

    parts = []
    for gi in range(len(POOL_WINDOWS)):
        cols = slice(gi * PGRP, (gi + 1) * PGRP)
        parts.append(_dot(yp_ref[:, cols].astype(BF16), pw_ref[gi]))
    yp = jnp.concatenate(parts, axis=1) * ps_ref[...]
    y_ref[...] = _dot(yp.astype(BF16), w_ref[...])


def _post_b(proj3, hist_s, pool_w, pool_scale, w_proj_b):
    xcol = 2 * D // PW
    return pl.pallas_call(
        _post_b_kernel,
        grid=(NPT + 1,),
        in_specs=[pl.BlockSpec((TM, PW), lambda i: (i, xcol)),
                  pl.BlockSpec((16, PW), lambda i: (jnp.maximum(i * (TM // 16) - 1, 0), xcol)),
                  pl.BlockSpec((PHIST * BS, PW), lambda i: (0, 0)),
                  pl.BlockSpec((len(POOL_WINDOWS), PGRP, PGRP), lambda i: (0, 0, 0)),
                  pl.BlockSpec((1, PW), lambda i: (0, 0)),
                  pl.BlockSpec((PW, D), lambda i: (0, 0))],
        out_specs=pl.BlockSpec((TM, D), lambda i: (i, 0)),
        out_shape=jax.ShapeDtypeStruct((R, D), F32),
        scratch_shapes=[pltpu.VMEM((TM + 16, PW), F32), pltpu.VMEM((TM, PW), F32)],
        compiler_params=_params("arbitrary"),
        name="branch_b_out",
    )(proj3, proj3, hist_s, pool_w, pool_scale, w_proj_b)


TO = 256


def _post_out_kernel(ga_ref, gb_ref, ya_ref, yb_ref, xp_ref, xs_ref,
                     gtp_ref, shp_ref, scp_ref, gts_ref, shs_ref, scs_ref, g2_ref, w_ref,
                     x1_ref, h_ref):
    i = pl.program_id(0)
    merged = _sigmoid(ga_ref[...]) * ya_ref[...] + _sigmoid(gb_ref[...]) * yb_ref[...]
    mix = _dot(merged.astype(BF16), w_ref[...])
    g2 = g2_ref[...]

    @pl.when(i < RP // TO)
    def _():
        x1 = xp_ref[...] + gtp_ref[0] * mix
        x1_ref[...] = x1
        h_ref[...] = (_rms(x1, g2) * (1.0 + scp_ref[0]) + shp_ref[0]).astype(BF16)

    @pl.when(i >= RP // TO)
    def _():
        for t in range(TO // BS):
            rows = slice(t * BS, (t + 1) * BS)
            x1 = xs_ref[rows, :] + gts_ref[...] * mix[rows]
            x1_ref[rows, :] = x1
            h_ref[rows, :] = (_rms(x1, g2) * (1.0 + scs_ref[...]) + shs_ref[...]).astype(BF16)


def _post_out(proj3, ya, yb, xp2, xs2, mod_p, mod_s, norm2_g, w_out):
    npt = RP // TO
    tps = TP // TO
    gtp, gts = _mod_specs(2, tps, npt)
    shp, shs = _mod_specs(3, tps, npt)
    scp, scs = _mod_specs(4, tps, npt)
    tile = pl.BlockSpec((TO, D), lambda i: (i, 0))
    return pl.pallas_call(
        _post_out_kernel,
        grid=(R // TO,),
        in_specs=[pl.BlockSpec((TO, D), lambda i: (i, 0)),
                  pl.BlockSpec((TO, D), lambda i: (i, 1)),
                  tile, tile,
                  pl.BlockSpec((TO, D), lambda i: (jnp.minimum(i, npt - 1), 0)),
                  pl.BlockSpec((TO, D), lambda i: (jnp.maximum(i - npt, 0), 0)),
                  gtp, shp, scp, gts, shs, scs,
                  pl.BlockSpec((1, D), lambda i: (0, 0)),
                  pl.BlockSpec((D, D), lambda i: (0, 0))],
        out_specs=[tile, tile],
        out_shape=[jax.ShapeDtypeStruct((R, D), F32), jax.ShapeDtypeStruct((R, D), BF16)],
        compiler_params=_params("arbitrary"),
        name="merge_out_norm2",
    )(proj3, proj3, ya, yb, xp2, xs2, mod_p, mod_p, mod_p, mod_s, mod_s, mod_s, norm2_g, w_out)


FF_TN = 512
FF_TK = 1408


def _ffn1_kernel(h_ref, wg_ref, wu_ref, a_ref):
    h = h_ref[...]
    a_ref[...] = (_silu(_dot(h, wg_ref[...])) * _dot(h, wu_ref[...])).astype(BF16)


def _ffn1(h, w_gate_up):
    nj = DFF // FF_TN
    return pl.pallas_call(
        _ffn1_kernel,
        grid=(nj, R // TM),
        in_specs=[pl.BlockSpec((TM, D), lambda j, i: (i, 0)),
                  pl.BlockSpec((D, FF_TN), lambda j, i: (0, j)),
                  pl.BlockSpec((D, FF_TN), lambda j, i: (0, nj + j))],
        out_specs=pl.BlockSpec((TM, FF_TN), lambda j, i: (i, j)),
        out_shape=jax.ShapeDtypeStruct((R, DFF), BF16),
        compiler_params=_params("arbitrary", "arbitrary"),
        name="ffn_gate_up",
    )(h, w_gate_up, w_gate_up)


def _ffn2_kernel(a_ref, w_ref, x1_ref, gtp_ref, gts_ref, fg_ref, yp_ref, ys_ref, acc_ref):
    i = pl.program_id(0)
    k = pl.program_id(1)
    nk = DFF // FF_TK

    @pl.when(k == 0)
    def _():
        acc_ref[...] = jnp.zeros_like(acc_ref)

    acc_ref[...] += _dot(a_ref[...], w_ref[...])
    fg = fg_ref[...]

    @pl.when((k == nk - 1) & (i < NPT))
    def _():
        yp_ref[...] = _rms(x1_ref[...] + gtp_ref[0] * acc_ref[...], fg)

    @pl.when((k == nk - 1) & (i == NPT))
    def _():
        for t in range(TS):
            rows = slice(t * BS, (t + 1) * BS)
            ys_ref[rows, :] = _rms(x1_ref[rows, :] + gts_ref[...] * acc_ref[rows, :], fg)


def _ffn2(act, w_down, x1, mod_p, mod_s, final_g):
    gtp, gts = _mod_specs(5, TPB, NPT)
    return pl.pallas_call(
        _ffn2_kernel,
        grid=(R // TM, DFF // FF_TK),
        in_specs=[pl.BlockSpec((TM, FF_TK), lambda i, k: (i, k)),
                  pl.BlockSpec((FF_TK, D), lambda i, k: (k, 0)),
                  pl.BlockSpec((TM, D), lambda i, k: (i, 0)),
                  gtp, gts,
                  pl.BlockSpec((1, D), lambda i, k: (0, 0))],
        out_specs=[pl.BlockSpec((TM, D), lambda i, k: (jnp.minimum(i, NPT - 1), 0)),
                   pl.BlockSpec((TM, D), lambda i, k: (0, 0))],
        out_shape=[jax.ShapeDtypeStruct((RP, D), F32), jax.ShapeDtypeStruct((RS, D), F32)],
        scratch_shapes=[pltpu.VMEM((TM, D), F32)],
        compiler_params=_params("arbitrary", "arbitrary"),
        name="ffn_down_final",
    )(act, w_down, x1, mod_p, mod_s, final_g)


def _to_time_major(x):
    return jnp.transpose(x, (1, 0, 2)).reshape(TS * BS, x.shape[-1])


def _to_batch_major(x):
    return jnp.transpose(x.reshape(TS, BS, x.shape[-1]), (1, 0, 2))


def _pad_lanes(v):
    return jnp.zeros((1, 128), F32).at[0, :NH].set(v.astype(F32))


def kernel(x_prompt, x_sample, c_prompt, c_sample, state_delta, state_conv, state_pool, w_ada, b_ada, norm1_g,
           w_in, conv_w, a_log, dt_bias, o_norm_g, pool_w, pool_scale, w_proj_a, w_proj_b, w_out, norm2_g,
           w_gate_up, w_down, final_g):
    assert w_ada.shape[0] == 1, "single layer"
    xp2 = x_prompt.reshape(RP, D)
    xs2 = _to_time_major(x_sample)

    c_all = jnp.concatenate([c_sample, c_prompt, jnp.zeros((4, D), F32)], axis=0)
    mod = _mod(c_all, w_ada[0].astype(BF16), b_ada[0].reshape(1, 6 * D))
    mod_s = mod[:BS]
    mod_p = mod[BS:BS + BP].reshape(BP, 1, 6 * D)

    u = _pre(xp2, xs2, mod_p, mod_s, norm1_g[0].reshape(1, D))

    w_in_b = w_in[0].astype(BF16)
    o_z = 4 * QKW
    o_pool = o_z + 2 * NH
    o_gate = o_pool + PW
    w_ab = jnp.zeros((D, 256), BF16)
    w_ab = w_ab.at[:, :NH].set(w_in_b[:, o_z:o_z + NH]).at[:, 128:128 + NH].set(w_in_b[:, o_z + NH:o_pool])
    w_tail = jnp.concatenate([w_in_b[:, o_gate:], w_in_b[:, o_pool:o_gate]], axis=1)
    proj1 = _matmul(u, w_in_b, 4 * QKW, 1024, "in_proj_qkvz")
    proj_ab = _matmul(u, w_ab, 256, 256, "in_proj_ab")
    proj3 = _matmul(u, w_tail, 2 * D + PW, 1024, "in_proj_gates_pool")

    cw = conv_w[0]
    alog = _pad_lanes(a_log[0])
    dtb = _pad_lanes(dt_bias[0])

    m_p, qk_p, vb_p, kbd_p, qd_p, kt_p, cd_p = _prep_prompt(proj1, proj_ab, cw, alog, dtb)
    t_p = _solve(jnp.transpose(m_p, (1, 2, 0)), "tri_solve_prompt")
    t_p = jnp.transpose(t_p, (2, 0, 1)).astype(BF16)
    o_p, s_p = _scan_prompt(t_p, qk_p, vb_p, kbd_p, qd_p, kt_p, cd_p)

    qkv_s = _to_batch_major(proj1[RP:, :CONVC])
    conv_in = jnp.concatenate([state_conv[0], qkv_s, jnp.zeros((BS, SPAD - TS - (CONVW - 1), CONVC), F32)], axis=1)
    ab_s = _to_batch_major(proj_ab[RP:])
    ab_s = jnp.concatenate([ab_s, jnp.zeros((BS, SPAD - TS, 256), F32)], axis=1)
    m_s, qk_s, vb_s, kbd_s, qd_s, kt_s, cd_s = _prep_sample(
        conv_in.reshape(BS * SPAD, CONVC), ab_s.reshape(BS * SPAD, 256), cw, alog, dtb)
    nblk = BS // SSEQ
    m_blocks = jnp.diagonal(m_s.reshape(nblk * NH, SSEQ, SPAD, SSEQ, SPAD), axis1=1, axis2=3)
    m_blocks = jnp.transpose(m_blocks, (1, 2, 0, 3)).reshape(SPAD, SPAD, nblk * NH * SSEQ)
    t_blocks = _solve(m_blocks, "tri_solve_sample").reshape(SPAD, SPAD, nblk * NH, SSEQ)
    t_s = jnp.einsum("ijgb,bc->gbicj", t_blocks, jnp.eye(SSEQ, dtype=F32))
    t_s = t_s.reshape(nblk * NH, SSEQ * SPAD, SSEQ * SPAD).astype(BF16)
    o_s8, s_s = _scan_sample(t_s, qk_s, vb_s, kbd_s, qd_s, kt_s, cd_s, state_delta[0])
    o_s = _to_time_major(o_s8.reshape(BS, SPAD, QKW)[:, :TS])

    ya = _post_a(o_p, o_s, proj1, o_norm_g[0].reshape(1, DH), w_proj_a[0].astype(BF16))
    hist_s = jnp.transpose(state_pool[0], (1, 0, 2)).reshape(PHIST * BS, PW)
    yb = _post_b(proj3, hist_s, pool_w[0].astype(BF16), pool_scale[0].reshape(1, PW), w_proj_b[0].astype(BF16))
    x1, h = _post_out(proj3, ya, yb, xp2, xs2, mod_p, mod_s, norm2_g[0].reshape(1, D), w_out[0].astype(BF16))
    act = _ffn1(h, w_gate_up[0].astype(BF16))
    y_p, y_s = _ffn2(act, w_down[0].astype(BF16), x1, mod_p, mod_s, final_g.reshape(1, D))

    xpool_s = _to_batch_major(proj3[RP:, 2 * D:])
    return (y_p.reshape(BP, TP, D),
            _to_batch_major(y_s),
            s_p[None],
            proj1[:RP, :CONVC].reshape(BP, TP, CONVC)[:, TP - (CONVW - 1):][None],
            proj3[:RP, 2 * D:].reshape(BP, TP, PW)[:, TP - PHIST:][None],
            s_s[None],
            conv_in[:, TS:TS + CONVW - 1][None],
            jnp.concatenate([state_pool[0][:, TS:], xpool_s], axis=1)[None])
```

```python
import functools

import jax
import jax.numpy as jnp
from jax import lax
from jax.experimental import pallas as pl
from jax.experimental.pallas import tpu as pltpu

F32 = jnp.float32
BF16 = jnp.bfloat16

D = 2048
NH = 8
DH = 128
QKW = NH * DH
CONVC = 3 * QKW
CONVW = 4
PW = 1024
PGRP = 256
POOL_WINDOWS = (2, 4, 8, 16)
PHIST = 15
DFF = 5632
EPS = 1e-6

BP, TP = 4, 2048
BS, TS = 128, 4
RP = BP * TP
RS = BS * TS
R = RP + RS
TM = 512
NPT = RP // TM
TPB = TP // TM
CHUNK = 64
SPAD = 8
SSEQ = 16

VMEM_LIMIT = 56 * 1024 * 1024

NT_DIMS = (((1,), (1,)), ((), ()))
TN_DIMS = (((0,), (0,)), ((), ()))


def _params(*sem):
    return pltpu.CompilerParams(dimension_semantics=sem, vmem_limit_bytes=VMEM_LIMIT)


def _sigmoid(x):
    return 1.0 / (1.0 + jnp.exp(-x))


def _silu(x):
    return x * _sigmoid(x)


def _softplus(x):
    return jnp.maximum(x, 0.0) + jnp.log1p(jnp.exp(-jnp.abs(x)))


def _rms(x, gain):
    return x * lax.rsqrt(jnp.mean(x * x, axis=-1, keepdims=True) + EPS) * gain


def _dot(a, b):
    return jnp.dot(a, b, preferred_element_type=F32)


def _mod_kernel(c_ref, w_ref, b_ref, o_ref):
    c = c_ref[...]
    o_ref[...] = _dot(_silu(c).astype(BF16), w_ref[...].astype(BF16)) + b_ref[...]


def _mod(c_all, w_ada, b_ada):
    n = c_all.shape[0]
    tn = 1024
    return pl.pallas_call(
        _mod_kernel,
        grid=(6 * D // tn,),
        in_specs=[pl.BlockSpec((n, D), lambda j: (0, 0)),
                  pl.BlockSpec((D, tn), lambda j: (0, j)),
                  pl.BlockSpec((1, tn), lambda j: (0, j))],
        out_specs=pl.BlockSpec((n, tn), lambda j: (0, j)),
        out_shape=jax.ShapeDtypeStruct((n, 6 * D), F32),
        compiler_params=_params("arbitrary"),
        name="ada_mod",
    )(c_all, w_ada, b_ada)


def _mod_specs(col, tiles_per_seq, n_prompt_tiles):
    p = pl.BlockSpec((1, 1, D), lambda i, *_: (jnp.minimum(i // tiles_per_seq, BP - 1), 0, col))
    s = pl.BlockSpec((BS, D), lambda i, *_: (0, col))
    return p, s


def _pre_kernel(xp_ref, xs_ref, shp_ref, scp_ref, shs_ref, scs_ref, g_ref, u_ref):
    i = pl.program_id(0)
    g = g_ref[...]

    @pl.when(i < NPT)
    def _():
        u = _rms(xp_ref[...], g) * (1.0 + scp_ref[0]) + shp_ref[0]
        u_ref[...] = u.astype(BF16)

    @pl.when(i == NPT)
    def _():
        for t in range(TS):
            rows = slice(t * BS, (t + 1) * BS)
            u = _rms(xs_ref[rows, :], g) * (1.0 + scs_ref[...]) + shs_ref[...]
            u_ref[rows, :] = u.astype(BF16)


def _pre(xp2, xs2, mod_p, mod_s, g):
    shp, shs = _mod_specs(0, TPB, NPT)
    scp, scs = _mod_specs(1, TPB, NPT)
    return pl.pallas_call(
        _pre_kernel,
        grid=(NPT + 1,),
        in_specs=[pl.BlockSpec((TM, D), lambda i: (jnp.minimum(i, NPT - 1), 0)),
                  pl.BlockSpec((TM, D), lambda i: (0, 0)),
                  shp, scp, shs, scs,
                  pl.BlockSpec((1, D), lambda i: (0, 0))],
        out_specs=pl.BlockSpec((TM, D), lambda i: (i, 0)),
        out_shape=jax.ShapeDtypeStruct((R, D), BF16),
        compiler_params=_params("arbitrary"),
        name="norm1_mod",
    )(xp2, xs2, mod_p, mod_p, mod_s, mod_s, g)


def _mm_kernel(a_ref, w_ref, o_ref, wb_ref):
    @pl.when(pl.program_id(1) == 0)
    def _():
        wb_ref[...] = w_ref[...].astype(BF16)

    o_ref[...] = _dot(a_ref[...], wb_ref[...])


def _matmul(a, w, n_cols, tn, name):
    rows, k = a.shape
    return pl.pallas_call(
        _mm_kernel,
        grid=(n_cols // tn, rows // TM),
        in_specs=[pl.BlockSpec((TM, k), lambda j, i: (i, 0)),
                  pl.BlockSpec((k, tn), lambda j, i: (0, j))],
        out_specs=pl.BlockSpec((TM, tn), lambda j, i: (i, j)),
        out_shape=jax.ShapeDtypeStruct((rows, n_cols), F32),
        scratch_shapes=[pltpu.VMEM((k, tn), BF16)],
        compiler_params=_params("arbitrary", "arbitrary"),
        name=name,
    )(a, w)


def _mm_shift_kernel(shift, a_ref, w0_ref, w1_ref, o_ref, wb_ref):
    @pl.when(pl.program_id(1) == 0)
    def _():
        w = jnp.concatenate([w0_ref[:, shift:], w1_ref[:, :shift]], axis=1)
        wb_ref[...] = w.astype(BF16)

    o_ref[...] = _dot(a_ref[...], wb_ref[...])


def _matmul_shifted(a, w, col0, shift, nt, rot, tn, name):
    rows, k = a.shape
    src = lambda j: lax.rem(j + rot, nt)
    return pl.pallas_call(
        functools.partial(_mm_shift_kernel, shift),
        grid=(nt, rows // TM),
        in_specs=[pl.BlockSpec((TM, k), lambda j, i: (i, 0)),
                  pl.BlockSpec((k, tn), lambda j, i: (0, col0 // tn + src(j))),
                  pl.BlockSpec((k, 128), lambda j, i: (0, (col0 + (src(j) + 1) * tn) // 128))],
        out_specs=pl.BlockSpec((TM, tn), lambda j, i: (i, j)),
        out_shape=jax.ShapeDtypeStruct((rows, nt * tn), F32),
        scratch_shapes=[pltpu.VMEM((k, tn), BF16)],
        compiler_params=_params("arbitrary", "arbitrary"),
        name=name,
    )(a, w, w)


def _delta_prep(get_qkv, ab, alog, dtb, valid, chunk, seq_block, m_ref, l2_ref, vb_ref, kbd_ref, qd_ref, cd_ref):
    rows = ab.shape[0]
    a = ab[:, :128]
    b = ab[:, 128:]
    beta = _sigmoid(b)
    g = -jnp.exp(alog) * _softplus(a + dtb)
    if valid is not None:
        beta = jnp.where(valid, beta, 0.0)
        g = jnp.where(valid, g, 0.0)
    shift = seq_block.bit_length() - 1
    r = lax.broadcasted_iota(jnp.int32, (rows, rows), 0)
    c = lax.broadcasted_iota(jnp.int32, (rows, rows), 1)
    same = (r >> shift) == (c >> shift)
    ltri = jnp.where(same & (r >= c), 1.0, 0.0)
    lall = jnp.where(same, 1.0, 0.0)
    dcol = jnp.dot(ltri, g, preferred_element_type=F32, precision=lax.Precision.HIGHEST)
    last = jnp.dot(lall, g, preferred_element_type=F32, precision=lax.Precision.HIGHEST)
    cd_ref[...] = jnp.exp(last)
    drow = dcol.T
    edc = jnp.exp(dcol)
    etl = jnp.exp(last - dcol)
    rc = lax.broadcasted_iota(jnp.int32, (chunk, chunk), 0)
    cc = lax.broadcasted_iota(jnp.int32, (chunk, chunk), 1)
    same_c = (rc >> shift) == (cc >> shift)
    causal = same_c & (rc >= cc)
    strict = same_c & (rc > cc)
    for h in range(NH):
        q, k, v = get_qkv(h)
        qn = q * lax.rsqrt(jnp.sum(q * q, axis=-1, keepdims=True) + EPS) * (DH ** -0.5)
        kn = k * lax.rsqrt(jnp.sum(k * k, axis=-1, keepdims=True) + EPS)
        bcol = beta[:, h:h + 1]
        kb = kn * bcol
        cols = slice(h * DH, (h + 1) * DH)
        vb_ref[:, cols] = (v * bcol).astype(BF16)
        kbd_ref[:, cols] = (kb * edc[:, h:h + 1]).astype(BF16)
        qd_ref[:, cols] = (qn * edc[:, h:h + 1]).astype(BF16)
        kt = kn * etl[:, h:h + 1]
        for sc in range(rows // chunk):
            rs = slice(sc * chunk, (sc + 1) * chunk)
            diff = dcol[rs, h:h + 1] - drow[h:h + 1, rs]
            lm = jnp.where(causal, jnp.exp(jnp.where(causal, diff, 0.0)), 0.0)
            lhs = jnp.concatenate([kb[rs], qn[rs]], axis=0).astype(BF16)
            s = lax.dot_general(lhs, kn[rs].astype(BF16), NT_DIMS, preferred_element_type=F32)
            idx = sc * NH + h
            m_ref[idx] = jnp.where(strict, s[:chunk] * lm, 0.0)
            l2_ref[idx] = jnp.concatenate([s[chunk:] * lm, kt[rs].T], axis=0).astype(BF16)


def _prep_prompt_kernel(cur_ref, prev_ref, ab_ref, cw_ref, alog_ref, dtb_ref,
                        m_ref, l2_ref, vb_ref, kbd_ref, qd_ref, cd_ref, xc_ref):
    blk = pl.program_id(1)
    xc_ref[0:8, :] = jnp.where(blk > 0, prev_ref[...], 0.0)
    xc_ref[8:8 + TM, :] = cur_ref[...]
    w = cw_ref[...]

    def conv(cols):
        y = xc_ref[8:8 + TM, cols] * w[CONVW - 1:CONVW, cols]
        for j in range(CONVW - 1):
            y = y + xc_ref[5 + j:5 + j + TM, cols] * w[j:j + 1, cols]
        return _silu(y)

    def get_qkv(h):
        return tuple(conv(slice(p * QKW + h * DH, p * QKW + (h + 1) * DH)) for p in range(3))

    _delta_prep(get_qkv, ab_ref[...], alog_ref[...], dtb_ref[...], None, CHUNK, CHUNK,
                m_ref, l2_ref, vb_ref, kbd_ref, qd_ref, cd_ref)


def _prep_prompt(proj1, proj_ab, conv_w, alog, dtb):
    cpb = TM // CHUNK * NH
    nch = RP // CHUNK * NH
    row = lambda b, k: b * TPB + k
    tok = pl.BlockSpec((TM, QKW), lambda b, k: (row(b, k), 0))
    return pl.pallas_call(
        _prep_prompt_kernel,
        grid=(BP, TPB),
        in_specs=[pl.BlockSpec((TM, CONVC), lambda b, k: (row(b, k), 0)),
                  pl.BlockSpec((8, CONVC), lambda b, k: (jnp.maximum(row(b, k) * (TM // 8) - 1, 0), 0)),
                  pl.BlockSpec((TM, 256), lambda b, k: (row(b, k), 0)),
                  pl.BlockSpec((CONVW, CONVC), lambda b, k: (0, 0)),
                  pl.BlockSpec((1, 128), lambda b, k: (0, 0)),
                  pl.BlockSpec((1, 128), lambda b, k: (0, 0))],
        out_specs=[pl.BlockSpec((cpb, CHUNK, CHUNK), lambda b, k: (row(b, k), 0, 0)),
                   pl.BlockSpec((cpb, CHUNK + DH, CHUNK), lambda b, k: (row(b, k), 0, 0)),
                   tok, tok, tok,
                   pl.BlockSpec((TM, 128), lambda b, k: (row(b, k), 0))],
        out_shape=[jax.ShapeDtypeStruct((nch, CHUNK, CHUNK), F32),
                   jax.ShapeDtypeStruct((nch, CHUNK + DH, CHUNK), BF16),
                   jax.ShapeDtypeStruct((RP, QKW), BF16),
                   jax.ShapeDtypeStruct((RP, QKW), BF16),
                   jax.ShapeDtypeStruct((RP, QKW), BF16),
                   jax.ShapeDtypeStruct((RP, 128), F32)],
        scratch_shapes=[pltpu.VMEM((TM + 8, CONVC), F32)],
        compiler_params=_params("arbitrary", "arbitrary"),
        name="delta_prep_prompt",
    )(proj1, proj1, proj_ab, conv_w, alog, dtb)


def _prep_sample_kernel(xp_ref, ab_ref, cw_ref, alog_ref, dtb_ref,
                        m_ref, l2_ref, vb_ref, kbd_ref, qd_ref, cd_ref):
    nrow = SSEQ * SPAD
    w = cw_ref[...]
    valid = (lax.broadcasted_iota(jnp.int32, (nrow, 1), 0) & (SPAD - 1)) < TS

    def conv(cols):
        x = xp_ref[:, cols]
        y = x * w[0:1, cols]
        for j in range(1, CONVW):
            y = y + pltpu.roll(x, nrow - j, axis=0) * w[j:j + 1, cols]
        return jnp.where(valid, _silu(y), 0.0)

    def get_qkv(h):
        return tuple(conv(slice(p * QKW + h * DH, p * QKW + (h + 1) * DH)) for p in range(3))

    _delta_prep(get_qkv, ab_ref[...], alog_ref[...], dtb_ref[...], valid, nrow, SPAD,
                m_ref, l2_ref, vb_ref, kbd_ref, qd_ref, cd_ref)


def _prep_sample(xp8, ab8, conv_w, alog, dtb):
    nrow = SSEQ * SPAD
    steps = BS // SSEQ
    tok = pl.BlockSpec((nrow, QKW), lambda s: (s, 0))
    mat = pl.BlockSpec((NH, nrow, nrow), lambda s: (s, 0, 0))
    return pl.pallas_call(
        _prep_sample_kernel,
        grid=(steps,),
        in_specs=[pl.BlockSpec((nrow, CONVC), lambda s: (s, 0)),
                  pl.BlockSpec((nrow, 256), lambda s: (s, 0)),
                  pl.BlockSpec((CONVW, CONVC), lambda s: (0, 0)),
                  pl.BlockSpec((1, 128), lambda s: (0, 0)),
                  pl.BlockSpec((1, 128), lambda s: (0, 0))],
        out_specs=[mat, pl.BlockSpec((NH, nrow + DH, nrow), lambda s: (s, 0, 0)),
                   tok, tok, tok, pl.BlockSpec((nrow, 128), lambda s: (s, 0))],
        out_shape=[jax.ShapeDtypeStruct((steps * NH, nrow, nrow), F32),
                   jax.ShapeDtypeStruct((steps * NH, nrow + DH, nrow), BF16),
                   jax.ShapeDtypeStruct((BS * SPAD, QKW), BF16),
                   jax.ShapeDtypeStruct((BS * SPAD, QKW), BF16),
                   jax.ShapeDtypeStruct((BS * SPAD, QKW), BF16),
                   jax.ShapeDtypeStruct((BS * SPAD, 128), F32)],
        compiler_params=_params("arbitrary"),
        name="delta_prep_sample",
    )(xp8, ab8, conv_w, alog, dtb)


def _solve_kernel(m_ref, x_ref):
    n, _, lanes = m_ref.shape
    rowid = lax.broadcasted_iota(jnp.int32, (n, lanes), 0)

    def body_i(i, carry):
        def body_j(j, acc):
            return acc - m_ref[i, pl.ds(j, 1), :] * x_ref[j]

        x_ref[i] = lax.fori_loop(0, i, body_j, (rowid == i).astype(F32))
        return carry

    lax.fori_loop(0, n, body_i, 0)


def _solve(m, name):
    n, _, g = m.shape
    lanes = 128
    spec = pl.BlockSpec((n, n, lanes), lambda s: (0, 0, s))
    return pl.pallas_call(
        _solve_kernel,
        grid=(g // lanes,),
        in_specs=[spec],
        out_specs=spec,
        out_shape=jax.ShapeDtypeStruct((n, n, g), F32),
        compiler_params=_params("arbitrary"),
        name=name,
    )(m)


CPS = 2


def _scan_prompt_kernel(t_ref, l2_ref, vb_ref, kbd_ref, qd_ref, cd_ref, o_ref, sout_ref, s_ref, ub_ref, wd_ref):
    step = pl.program_id(0)

    @pl.when(step == 0)
    def _():
        s_ref[...] = jnp.zeros_like(s_ref)

    chains = [(b, h) for b in range(BP) for h in range(NH)]
    for c in range(CPS):
        rows = slice(c * CHUNK, (c + 1) * CHUNK)
        for n, (b, h) in enumerate(chains):
            cols = slice(h * DH, (h + 1) * DH)
            rhs = jnp.concatenate([vb_ref[b, rows, cols], kbd_ref[b, rows, cols]], axis=1)
            r1 = _dot(t_ref[b, c * NH + h], rhs)
            ub_ref[c * len(chains) + n] = r1[:, :DH]
            wd_ref[c * len(chains) + n] = r1[:, DH:].astype(BF16)
    for c in range(CPS):
        rows = slice(c * CHUNK, (c + 1) * CHUNK)
        r2s = []
        for n, (b, h) in enumerate(chains):
            cols = slice(h * DH, (h + 1) * DH)
            lhs = jnp.concatenate([wd_ref[c * len(chains) + n], qd_ref[b, rows, cols]], axis=0)
            r2s.append(_dot(lhs, s_ref[n].astype(BF16)))
        for n, (b, h) in enumerate(chains):
            cols = slice(h * DH, (h + 1) * DH)
            r2 = r2s[n]
            u = (ub_ref[c * len(chains) + n] - r2[:CHUNK]).astype(BF16)
            r3 = _dot(l2_ref[b, c * NH + h], u)
            o_ref[b, rows, cols] = r2[CHUNK:] + r3[:CHUNK]
            cd = cd_ref[b, c * CHUNK:c * CHUNK + 1, h:h + 1]
            s_ref[n] = s_ref[n] * cd + r3[CHUNK:]

    @pl.when(step == pl.num_programs(0) - 1)
    def _():
        sout_ref[...] = s_ref[...]


def _scan_prompt(t, l2, vb, kbd, qd, cd):
    nchunk = TP // CHUNK
    rows = CPS * CHUNK
    t = t.reshape(BP, nchunk * NH, CHUNK, CHUNK)
    l2 = l2.reshape(BP, nchunk * NH, CHUNK + DH, CHUNK)
    tok3 = lambda a: a.reshape(BP, TP, a.shape[-1])
    tok = pl.BlockSpec((BP, rows, QKW), lambda s: (0, s, 0))
    o, s_fin = pl.pallas_call(
        _scan_prompt_kernel,
        grid=(nchunk // CPS,),
        in_specs=[pl.BlockSpec((BP, CPS * NH, CHUNK, CHUNK), lambda s: (0, s, 0, 0)),
                  pl.BlockSpec((BP, CPS * NH, CHUNK + DH, CHUNK), lambda s: (0, s, 0, 0)),
                  tok, tok, tok,
                  pl.BlockSpec((BP, rows, 128), lambda s: (0, s, 0))],
        out_specs=[tok, pl.BlockSpec((BP * NH, DH, DH), lambda s: (0, 0, 0))],
        out_shape=[jax.ShapeDtypeStruct((BP, TP, QKW), F32),
                   jax.ShapeDtypeStruct((BP * NH, DH, DH), F32)],
        scratch_shapes=[pltpu.VMEM((BP * NH, DH, DH), F32),
                        pltpu.VMEM((CPS * BP * NH, CHUNK, DH), F32),
                        pltpu.VMEM((CPS * BP * NH, CHUNK, DH), BF16)],
        compiler_params=_params("arbitrary"),
        name="delta_scan_prompt",
    )(t, l2, tok3(vb), tok3(kbd), tok3(qd), tok3(cd))
    return o.reshape(RP, QKW), s_fin.reshape(BP, NH, DH, DH)


def _scan_sample_kernel(t_ref, l2_ref, vb_ref, kbd_ref, qd_ref, cd_ref, s0_ref, o_ref, sout_ref):
    nrow = SSEQ * SPAD
    seq_of_row = lax.broadcasted_iota(jnp.int32, (nrow, 1), 0) >> (SPAD.bit_length() - 1)
    for h in range(NH):
        cols = slice(h * DH, (h + 1) * DH)
        rhs = jnp.concatenate([vb_ref[:, cols], kbd_ref[:, cols]], axis=1)
        r1 = _dot(t_ref[h], rhs)
        qd = qd_ref[:, cols].astype(F32)
        u_parts, qs_parts = [], []
        for b in range(SSEQ):
            rows = slice(b * SPAD, (b + 1) * SPAD)
            lhs = jnp.concatenate([r1[rows, DH:], qd[rows]], axis=0).astype(BF16)
            r2 = _dot(lhs, s0_ref[b, h].astype(BF16))
            u_parts.append(r1[rows, :DH] - r2[:SPAD])
            qs_parts.append(r2[SPAD:])
        u = jnp.concatenate(u_parts, axis=0)
        o_ref[:, cols] = jnp.concatenate(qs_parts, axis=0) + _dot(l2_ref[h, :nrow, :], u.astype(BF16))
        ktt = l2_ref[h, nrow:, :]
        for b in range(SSEQ):
            ub = jnp.where(seq_of_row == b, u, 0.0).astype(BF16)
            cd = cd_ref[b * SPAD:b * SPAD + 1, h:h + 1]
            sout_ref[b, h] = s0_ref[b, h] * cd + _dot(ktt, ub)


def _scan_sample(t, l2, vb, kbd, qd, cd, s0):
    nrow = SSEQ * SPAD
    steps = BS // SSEQ
    tok = pl.BlockSpec((nrow, QKW), lambda s: (s, 0))
    st = pl.BlockSpec((SSEQ, NH, DH, DH), lambda s: (s, 0, 0, 0))
    return pl.pallas_call(
        _scan_sample_kernel,
        grid=(steps,),
        in_specs=[pl.BlockSpec((NH, nrow, nrow), lambda s: (s, 0, 0)),
                  pl.BlockSpec((NH, nrow + DH, nrow), lambda s: (s, 0, 0)),
                  tok, tok, tok, pl.BlockSpec((nrow, 128), lambda s: (s, 0)), st],
        out_specs=[tok, st],
        out_shape=[jax.ShapeDtypeStruct((BS * SPAD, QKW), F32),
                   jax.ShapeDtypeStruct((BS, NH, DH, DH), F32)],
        compiler_params=_params("arbitrary"),
        name="delta_scan_sample",
    )(t, l2, vb, kbd, qd, cd, s0)


def _post_a_kernel(op_ref, os_ref, z_ref, g_ref, w_ref, y_ref):
    i = pl.program_id(0)
    o = jnp.where(i < NPT, op_ref[...], os_ref[...])
    g = g_ref[...]
    parts = []
    for h in range(NH):
        cols = slice(h * DH, (h + 1) * DH)
        parts.append((_rms(o[:, cols], g) * _silu(z_ref[:, cols])).astype(BF16))
    y_ref[...] = _dot(jnp.concatenate(parts, axis=1), w_ref[...])


def _post_a(o_p, o_s, proj1, o_norm_g, w_proj_a):
    return pl.pallas_call(
        _post_a_kernel,
        grid=(NPT + 1,),
        in_specs=[pl.BlockSpec((TM, QKW), lambda i: (jnp.minimum(i, NPT - 1), 0)),
                  pl.BlockSpec((TM, QKW), lambda i: (0, 0)),
                  pl.BlockSpec((TM, QKW), lambda i: (i, 3)),
                  pl.BlockSpec((1, DH), lambda i: (0, 0)),
                  pl.BlockSpec((QKW, D), lambda i: (0, 0))],
        out_specs=pl.BlockSpec((TM, D), lambda i: (i, 0)),
        out_shape=jax.ShapeDtypeStruct((R, D), F32),
        compiler_params=_params("arbitrary"),
        name="branch_a_out",
    )(o_p, o_s, proj1, o_norm_g, w_proj_a)


def _post_b_kernel(x_ref, prev_ref, hist_ref, pw_ref, ps_ref, w_ref, y_ref, xc_ref, yp_ref):
    i = pl.program_id(0)
    hist_rows = 16

    @pl.when(i < NPT)
    def _():
        k = i % TPB
        xc_ref[0:hist_rows, :] = jnp.where(k > 0, prev_ref[...], 0.0)
        xc_ref[hist_rows:hist_rows + TM, :] = x_ref[...]
        pos = k * TM + lax.broadcasted_iota(jnp.int32, (TM, 1), 0)
        for gi, win in enumerate(POOL_WINDOWS):
            cols = slice(gi * PGRP, (gi + 1) * PGRP)
            acc = x_ref[:, cols]
            for d in range(1, win):
                acc = acc + xc_ref[hist_rows - d:hist_rows - d + TM, cols]
            cnt = jnp.minimum(win, pos + 1).astype(F32)
            yp_ref[:, cols] = acc / cnt - x_ref[:, cols]

    @pl.when(i == NPT)
    def _():
        for gi, win in enumerate(POOL_WINDOWS):
            cols = slice(gi * PGRP, (gi + 1) * PGRP)
            for t in range(TS):
                rows = slice(t * BS, (t + 1) * BS)
                acc = x_ref[rows, cols]
                for d in range(1, win):
                    src = PHIST + t - d
                    if src >= PHIST:
                        acc = acc + x_ref[(src - PHIST) * BS:(src - PHIST + 1) * BS, cols]
                    else:
                        acc = acc + hist_ref[src * BS:(src + 1) * BS, cols]
                yp_ref[rows, cols] = acc / float(win) - x_ref[rows, cols]

    parts = []
    for gi in range(len(POOL_WINDOWS)):
        cols = slice(gi * PGRP, (gi + 1) * PGRP)
        parts.append(_dot(yp_ref[:, cols].astype(BF16), pw_ref[gi]))
    yp = jnp.concatenate(parts, axis=1) * ps_ref[...]
    y_ref[...] = _dot(yp.astype(BF16), w_ref[...])


def _post_b(proj3, hist_s, pool_w, pool_scale, w_proj_b):
    xcol = 2 * D // PW
    return pl.pallas_call(
        _post_b_kernel,
        grid=(NPT + 1,),
        in_specs=[pl.BlockSpec((TM, PW), lambda i: (i, xcol)),
                  pl.BlockSpec((16, PW), lambda i: (jnp.maximum(i * (TM // 16) - 1, 0), xcol)),
                  pl.BlockSpec((PHIST * BS, PW), lambda i: (0, 0)),
                  pl.BlockSpec((len(POOL_WINDOWS), PGRP, PGRP), lambda i: (0, 0, 0)),
                  pl.BlockSpec((1, PW), lambda i: (0, 0)),
                  pl.BlockSpec((PW, D), lambda i: (0, 0))],
        out_specs=pl.BlockSpec((TM, D), lambda i: (i, 0)),
        out_shape=jax.ShapeDtypeStruct((R, D), F32),
        scratch_shapes=[pltpu.VMEM((TM + 16, PW), F32), pltpu.VMEM((TM, PW), F32)],
        compiler_params=_params("arbitrary"),
        name="branch_b_out",
    )(proj3, proj3, hist_s, pool_w, pool_scale, w_proj_b)


TO = 256


def _post_out_kernel(ga_ref, gb_ref, ya_ref, yb_ref, xp_ref, xs_ref,
                     gtp_ref, shp_ref, scp_ref, gts_ref, shs_ref, scs_ref, g2_ref, w_ref,
                     x1_ref, h_ref):
    i = pl.program_id(0)
    merged = _sigmoid(ga_ref[...]) * ya_ref[...] + _sigmoid(gb_ref[...]) * yb_ref[...]
    mix = _dot(merged.astype(BF16), w_ref[...])
    g2 = g2_ref[...]

    @pl.when(i < RP // TO)
    def _():
        x1 = xp_ref[...] + gtp_ref[0] * mix
        x1_ref[...] = x1
        h_ref[...] = (_rms(x1, g2) * (1.0 + scp_ref[0]) + shp_ref[0]).astype(BF16)

    @pl.when(i >= RP // TO)
    def _():
        for t in range(TO // BS):
            rows = slice(t * BS, (t + 1) * BS)
            x1 = xs_ref[rows, :] + gts_ref[...] * mix[rows]
            x1_ref[rows, :] = x1
            h_ref[rows, :] = (_rms(x1, g2) * (1.0 + scs_ref[...]) + shs_ref[...]).astype(BF16)


def _post_out(proj3, ya, yb, xp2, xs2, mod_p, mod_s, norm2_g, w_out):
    npt = RP // TO
    tps = TP // TO
    gtp, gts = _mod_specs(2, tps, npt)
    shp, shs = _mod_specs(3, tps, npt)
    scp, scs = _mod_specs(4, tps, npt)
    tile = pl.BlockSpec((TO, D), lambda i: (i, 0))
    return pl.pallas_call(
        _post_out_kernel,
        grid=(R // TO,),
        in_specs=[pl.BlockSpec((TO, D), lambda i: (i, 0)),
                  pl.BlockSpec((TO, D), lambda i: (i, 1)),
                  tile, tile,
                  pl.BlockSpec((TO, D), lambda i: (jnp.minimum(i, npt - 1), 0)),
                  pl.BlockSpec((TO, D), lambda i: (jnp.maximum(i - npt, 0), 0)),
                  gtp, shp, scp, gts, shs, scs,
                  pl.BlockSpec((1, D), lambda i: (0, 0)),
                  pl.BlockSpec((D, D), lambda i: (0, 0))],
        out_specs=[tile, tile],
        out_shape=[jax.ShapeDtypeStruct((R, D), F32), jax.ShapeDtypeStruct((R, D), BF16)],
        compiler_params=_params("arbitrary"),
        name="merge_out_norm2",
    )(proj3, proj3, ya, yb, xp2, xs2, mod_p, mod_p, mod_p, mod_s, mod_s, mod_s, norm2_g, w_out)


FF_TN = 512
FF_TK = 1408


def _ffn1_kernel(h_ref, wg_ref, wu_ref, a_ref, wgb_ref, wub_ref):
    @pl.when(pl.program_id(1) == 0)
    def _():
        wgb_ref[...] = wg_ref[...].astype(BF16)
        wub_ref[...] = wu_ref[...].astype(BF16)

    h = h_ref[...]
    a_ref[...] = (_silu(_dot(h, wgb_ref[...])) * _dot(h, wub_ref[...])).astype(BF16)


def _ffn1(h, w_gate_up):
    nj = DFF // FF_TN
    return pl.pallas_call(
        _ffn1_kernel,
        grid=(nj, R // TM),
        in_specs=[pl.BlockSpec((TM, D), lambda j, i: (i, 0)),
                  pl.BlockSpec((D, FF_TN), lambda j, i: (0, j)),
                  pl.BlockSpec((D, FF_TN), lambda j, i: (0, nj + j))],
        out_specs=pl.BlockSpec((TM, FF_TN), lambda j, i: (i, j)),
        out_shape=jax.ShapeDtypeStruct((R, DFF), BF16),
        scratch_shapes=[pltpu.VMEM((D, FF_TN), BF16), pltpu.VMEM((D, FF_TN), BF16)],
        compiler_params=_params("arbitrary", "arbitrary"),
        name="ffn_gate_up",
    )(h, w_gate_up, w_gate_up)


def _ffn2_kernel(a_ref, w_ref, x1_ref, gtp_ref, gts_ref, fg_ref, yp_ref, ys_ref, acc_ref):
    i = pl.program_id(0)
    k = pl.program_id(1)
    nk = DFF // FF_TK

    @pl.when(k == 0)
    def _():
        acc_ref[...] = jnp.zeros_like(acc_ref)

    acc_ref[...] += _dot(a_ref[...], w_ref[...])
    fg = fg_ref[...]

    @pl.when((k == nk - 1) & (i < NPT))
    def _():
        yp_ref[...] = _rms(x1_ref[...] + gtp_ref[0] * acc_ref[...], fg)

    @pl.when((k == nk - 1) & (i == NPT))
    def _():
        for t in range(TS):
            rows = slice(t * BS, (t + 1) * BS)
            ys_ref[rows, :] = _rms(x1_ref[rows, :] + gts_ref[...] * acc_ref[rows, :], fg)


def _ffn2(act, w_down, x1, mod_p, mod_s, final_g):
    gtp, gts = _mod_specs(5, TPB, NPT)
    return pl.pallas_call(
        _ffn2_kernel,
        grid=(R // TM, DFF // FF_TK),
        in_specs=[pl.BlockSpec((TM, FF_TK), lambda i, k: (i, k)),
                  pl.BlockSpec((FF_TK, D), lambda i, k: (k, 0)),
                  pl.BlockSpec((TM, D), lambda i, k: (i, 0)),
                  gtp, gts,
                  pl.BlockSpec((1, D), lambda i, k: (0, 0))],
        out_specs=[pl.BlockSpec((TM, D), lambda i, k: (jnp.minimum(i, NPT - 1), 0)),
                   pl.BlockSpec((TM, D), lambda i, k: (0, 0))],
        out_shape=[jax.ShapeDtypeStruct((RP, D), F32), jax.ShapeDtypeStruct((RS, D), F32)],
        scratch_shapes=[pltpu.VMEM((TM, D), F32)],
        compiler_params=_params("arbitrary", "arbitrary"),
        name="ffn_down_final",
    )(act, w_down, x1, mod_p, mod_s, final_g)


def _to_time_major(x):
    return jnp.transpose(x, (1, 0, 2)).reshape(TS * BS, x.shape[-1])


def _to_batch_major(x):
    return jnp.transpose(x.reshape(TS, BS, x.shape[-1]), (1, 0, 2))


def _pad_lanes(v):
    return jnp.zeros((1, 128), F32).at[0, :NH].set(v.astype(F32))


def kernel(x_prompt, x_sample, c_prompt, c_sample, state_delta, state_conv, state_pool, w_ada, b_ada, norm1_g,
           w_in, conv_w, a_log, dt_bias, o_norm_g, pool_w, pool_scale, w_proj_a, w_proj_b, w_out, norm2_g,
           w_gate_up, w_down, final_g):
    assert w_ada.shape[0] == 1, "single layer"
    xp2 = x_prompt.reshape(RP, D)
    xs2 = _to_time_major(x_sample)

    c_all = jnp.concatenate([c_sample, c_prompt, jnp.zeros((4, D), F32)], axis=0)
    mod = _mod(c_all, w_ada[0], b_ada[0].reshape(1, 6 * D))
    mod_s = mod[:BS]
    mod_p = mod[BS:BS + BP].reshape(BP, 1, 6 * D)

    u = _pre(xp2, xs2, mod_p, mod_s, norm1_g[0].reshape(1, D))

    w_in0 = w_in[0]
    o_ab = 4 * QKW
    w_ab = jnp.zeros((D, 256), F32)
    w_ab = w_ab.at[:, :NH].set(w_in0[:, o_ab:o_ab + NH]).at[:, 128:128 + NH].set(w_in0[:, o_ab + NH:o_ab + 2 * NH])
    proj1 = _matmul(u, w_in0, 4 * QKW, 1024, "in_proj_qkvz")
    proj_ab = _matmul(u, w_ab, 256, 256, "in_proj_ab")
    proj3 = _matmul_shifted(u, w_in0, o_ab, 2 * NH, (PW + 2 * D) // 1024, PW // 1024, 1024, "in_proj_gates_pool")

    cw = conv_w[0]
    alog = _pad_lanes(a_log[0])
    dtb = _pad_lanes(dt_bias[0])

    m_p, l2_p, vb_p, kbd_p, qd_p, cd_p = _prep_prompt(proj1, proj_ab, cw, alog, dtb)
    t_p = _solve(jnp.transpose(m_p, (1, 2, 0)), "tri_solve_prompt")
    t_p = jnp.transpose(t_p, (2, 0, 1)).astype(BF16)
    o_p, s_p = _scan_prompt(t_p, l2_p, vb_p, kbd_p, qd_p, cd_p)

    qkv_s = _to_batch_major(proj1[RP:, :CONVC])
    conv_in = jnp.concatenate([state_conv[0], qkv_s, jnp.zeros((BS, SPAD - TS - (CONVW - 1), CONVC), F32)], axis=1)
    ab_s = _to_batch_major(proj_ab[RP:])
    ab_s = jnp.concatenate([ab_s, jnp.zeros((BS, SPAD - TS, 256), F32)], axis=1)
    m_s, l2_s, vb_s, kbd_s, qd_s, cd_s = _prep_sample(
        conv_in.reshape(BS * SPAD, CONVC), ab_s.reshape(BS * SPAD, 256), cw, alog, dtb)
    nblk = BS // SSEQ
    m_blocks = jnp.diagonal(m_s.reshape(nblk * NH, SSEQ, SPAD, SSEQ, SPAD), axis1=1, axis2=3)
    m_blocks = jnp.transpose(m_blocks, (1, 2, 0, 3)).reshape(SPAD, SPAD, nblk * NH * SSEQ)
    t_blocks = _solve(m_blocks, "tri_solve_sample").reshape(SPAD, SPAD, nblk * NH, SSEQ)
    t_s = jnp.einsum("ijgb,bc->gbicj", t_blocks, jnp.eye(SSEQ, dtype=F32))
    t_s = t_s.reshape(nblk * NH, SSEQ * SPAD, SSEQ * SPAD).astype(BF16)
    o_s8, s_s = _scan_sample(t_s, l2_s, vb_s, kbd_s, qd_s, cd_s, state_delta[0])
    o_s = _to_time_major(o_s8.reshape(BS, SPAD, QKW)[:, :TS])

    ya = _post_a(o_p, o_s, proj1, o_norm_g[0].reshape(1, DH), w_proj_a[0].astype(BF16))
    hist_s = jnp.transpose(state_pool[0], (1, 0, 2)).reshape(PHIST * BS, PW)
    yb = _post_b(proj3, hist_s, pool_w[0].astype(BF16), pool_scale[0].reshape(1, PW), w_proj_b[0].astype(BF16))
    x1, h = _post_out(proj3, ya, yb, xp2, xs2, mod_p, mod_s, norm2_g[0].reshape(1, D), w_out[0].astype(BF16))
    act = _ffn1(h, w_gate_up[0])
    y_p, y_s = _ffn2(act, w_down[0].astype(BF16), x1, mod_p, mod_s, final_g.reshape(1, D))

    xpool_s = _to_batch_major(proj3[RP:, 2 * D:])
    return (y_p.reshape(BP, TP, D),
            _to_batch_major(y_s),
            s_p[None],
            proj1[:RP, :CONVC].reshape(BP, TP, CONVC)[:, TP - (CONVW - 1):][None],
            proj3[:RP, 2 * D:].reshape(BP, TP, PW)[:, TP - PHIST:][None],
            s_s[None],
            conv_in[:, TS:TS + CONVW - 1][None],
            jnp.concatenate([state_pool[0][:, TS:], xpool_s], axis=1)[None])
```

```python
import functools

import jax
import jax.numpy as jnp
from jax import lax
from jax.experimental import pallas as pl
from jax.experimental.pallas import tpu as pltpu

F32 = jnp.float32
BF16 = jnp.bfloat16

D = 2048
NH = 8
DH = 128
QKW = NH * DH
CONVC = 3 * QKW
CONVW = 4
PW = 1024
PGRP = 256
POOL_WINDOWS = (2, 4, 8, 16)
PHIST = 15
DFF = 5632
EPS = 1e-6

BP, TP = 4, 2048
BS, TS = 128, 4
RP = BP * TP
RS = BS * TS
R = RP + RS
TM = 512
NPT = RP // TM
TPB = TP // TM
TMM = R // 8
CHUNK = 64
SPAD = 8
SSEQ = 16

VMEM_LIMIT = 56 * 1024 * 1024

NT_DIMS = (((1,), (1,)), ((), ()))


def _params(*sem):
    return pltpu.CompilerParams(dimension_semantics=sem, vmem_limit_bytes=VMEM_LIMIT)


def _sigmoid(x):
    return 1.0 / (1.0 + jnp.exp(-x))


def _silu(x):
    return x * _sigmoid(x)


def _softplus(x):
    return jnp.maximum(x, 0.0) + jnp.log1p(jnp.exp(-jnp.abs(x)))


def _rms(x, gain):
    return x * lax.rsqrt(jnp.mean(x * x, axis=-1, keepdims=True) + EPS) * gain


def _dot(a, b):
    return jnp.dot(a, b, preferred_element_type=F32)


def _mod_kernel(c_ref, w_ref, b_ref, o_ref):
    c = c_ref[...]
    o_ref[...] = _dot(_silu(c).astype(BF16), w_ref[...].astype(BF16)) + b_ref[...]


def _mod(c_all, w_ada, b_ada):
    n = c_all.shape[0]
    tn = 1024
    return pl.pallas_call(
        _mod_kernel,
        grid=(6 * D // tn,),
        in_specs=[pl.BlockSpec((n, D), lambda j: (0, 0)),
                  pl.BlockSpec((D, tn), lambda j: (0, j)),
                  pl.BlockSpec((1, tn), lambda j: (0, j))],
        out_specs=pl.BlockSpec((n, tn), lambda j: (0, j)),
        out_shape=jax.ShapeDtypeStruct((n, 6 * D), F32),
        compiler_params=_params("arbitrary"),
        name="ada_mod",
    )(c_all, w_ada, b_ada)


def _mod_specs(col, tiles_per_seq):
    p = pl.BlockSpec((1, 1, D), lambda i, *_: (jnp.minimum(i // tiles_per_seq, BP - 1), 0, col))
    s = pl.BlockSpec((BS, D), lambda i, *_: (0, col))
    return p, s


def _pre_kernel(xp_ref, xs_ref, shp_ref, scp_ref, shs_ref, scs_ref, g_ref, u_ref):
    i = pl.program_id(0)
    g = g_ref[...]

    @pl.when(i < NPT)
    def _():
        u = _rms(xp_ref[...], g) * (1.0 + scp_ref[0]) + shp_ref[0]
        u_ref[...] = u.astype(BF16)

    @pl.when(i == NPT)
    def _():
        for t in range(TS):
            rows = slice(t * BS, (t + 1) * BS)
            u = _rms(xs_ref[rows, :], g) * (1.0 + scs_ref[...]) + shs_ref[...]
            u_ref[rows, :] = u.astype(BF16)


def _pre(xp2, xs2, mod_p, mod_s, g):
    shp, shs = _mod_specs(0, TPB)
    scp, scs = _mod_specs(1, TPB)
    return pl.pallas_call(
        _pre_kernel,
        grid=(NPT + 1,),
        in_specs=[pl.BlockSpec((TM, D), lambda i: (jnp.minimum(i, NPT - 1), 0)),
                  pl.BlockSpec((TM, D), lambda i: (0, 0)),
                  shp, scp, shs, scs,
                  pl.BlockSpec((1, D), lambda i: (0, 0))],
        out_specs=pl.BlockSpec((TM, D), lambda i: (i, 0)),
        out_shape=jax.ShapeDtypeStruct((R, D), BF16),
        compiler_params=_params("arbitrary"),
        name="norm1_mod",
    )(xp2, xs2, mod_p, mod_p, mod_s, mod_s, g)


def _mm_kernel(a_ref, w_ref, o_ref, wb_ref):
    @pl.when(pl.program_id(1) == 0)
    def _():
        wb_ref[...] = w_ref[...].astype(BF16)

    o_ref[...] = _dot(a_ref[...], wb_ref[...])


def _matmul(a, w, n_cols, tn, name):
    rows, k = a.shape
    return pl.pallas_call(
        _mm_kernel,
        grid=(n_cols // tn, rows // TMM),
        in_specs=[pl.BlockSpec((TMM, k), lambda j, i: (i, 0)),
                  pl.BlockSpec((k, tn), lambda j, i: (0, j))],
        out_specs=pl.BlockSpec((TMM, tn), lambda j, i: (i, j)),
        out_shape=jax.ShapeDtypeStruct((rows, n_cols), F32),
        scratch_shapes=[pltpu.VMEM((k, tn), BF16)],
        compiler_params=_params("arbitrary", "arbitrary"),
        name=name,
    )(a, w)


def _mm_shift_kernel(shift, a_ref, w0_ref, w1_ref, o_ref, wb_ref):
    @pl.when(pl.program_id(1) == 0)
    def _():
        w = jnp.concatenate([w0_ref[:, shift:], w1_ref[:, :shift]], axis=1)
        wb_ref[...] = w.astype(BF16)

    o_ref[...] = _dot(a_ref[...], wb_ref[...])


def _matmul_shifted(a, w, col0, shift, nt, rot, tn, name):
    rows, k = a.shape
    src = lambda j: lax.rem(j + rot, nt)
    return pl.pallas_call(
        functools.partial(_mm_shift_kernel, shift),
        grid=(nt, rows // TMM),
        in_specs=[pl.BlockSpec((TMM, k), lambda j, i: (i, 0)),
                  pl.BlockSpec((k, tn), lambda j, i: (0, col0 // tn + src(j))),
                  pl.BlockSpec((k, 128), lambda j, i: (0, (col0 + (src(j) + 1) * tn) // 128))],
        out_specs=pl.BlockSpec((TMM, tn), lambda j, i: (i, j)),
        out_shape=jax.ShapeDtypeStruct((rows, nt * tn), F32),
        scratch_shapes=[pltpu.VMEM((k, tn), BF16)],
        compiler_params=_params("arbitrary", "arbitrary"),
        name=name,
    )(a, w, w)


def _delta_prep(get_qkv, ab, alog, dtb, valid, chunk, seq_block, m_ref, l2_ref, vb_ref, kbd_ref, qd_ref, cd_ref):
    rows = ab.shape[0]
    a = ab[:, :128]
    b = ab[:, 128:]
    beta = _sigmoid(b)
    g = -jnp.exp(alog) * _softplus(a + dtb)
    if valid is not None:
        beta = jnp.where(valid, beta, 0.0)
        g = jnp.where(valid, g, 0.0)
    shift = seq_block.bit_length() - 1
    r = lax.broadcasted_iota(jnp.int32, (rows, rows), 0)
    c = lax.broadcasted_iota(jnp.int32, (rows, rows), 1)
    same = (r >> shift) == (c >> shift)
    ltri = jnp.where(same & (r >= c), 1.0, 0.0)
    lall = jnp.where(same, 1.0, 0.0)
    dcol = jnp.dot(ltri, g, preferred_element_type=F32, precision=lax.Precision.HIGHEST)
    last = jnp.dot(lall, g, preferred_element_type=F32, precision=lax.Precision.HIGHEST)
    cd_ref[...] = jnp.exp(last)
    drow = dcol.T
    edc = jnp.exp(dcol)
    etl = jnp.exp(last - dcol)
    rc = lax.broadcasted_iota(jnp.int32, (chunk, chunk), 0)
    cc = lax.broadcasted_iota(jnp.int32, (chunk, chunk), 1)
    same_c = (rc >> shift) == (cc >> shift)
    causal = same_c & (rc >= cc)
    strict = same_c & (rc > cc)
    for h in range(NH):
        q, k, v = get_qkv(h)
        qn = q * lax.rsqrt(jnp.sum(q * q, axis=-1, keepdims=True) + EPS) * (DH ** -0.5)
        kn = k * lax.rsqrt(jnp.sum(k * k, axis=-1, keepdims=True) + EPS)
        bcol = beta[:, h:h + 1]
        kb = kn * bcol
        cols = slice(h * DH, (h + 1) * DH)
        vb_ref[:, cols] = (v * bcol).astype(BF16)
        kbd_ref[:, cols] = (kb * edc[:, h:h + 1]).astype(BF16)
        qd_ref[:, cols] = (qn * edc[:, h:h + 1]).astype(BF16)
        kt = kn * etl[:, h:h + 1]
        for sc in range(rows // chunk):
            rs = slice(sc * chunk, (sc + 1) * chunk)
            diff = dcol[rs, h:h + 1] - drow[h:h + 1, rs]
            lm = jnp.where(causal, jnp.exp(jnp.where(causal, diff, 0.0)), 0.0)
            lhs = jnp.concatenate([kb[rs], qn[rs]], axis=0).astype(BF16)
            s = lax.dot_general(lhs, kn[rs].astype(BF16), NT_DIMS, preferred_element_type=F32)
            idx = sc * NH + h
            m_ref[idx] = jnp.where(strict, s[:chunk] * lm, 0.0)
            l2_ref[idx] = jnp.concatenate([s[chunk:] * lm, kt[rs].T], axis=0).astype(BF16)


def _prep_prompt_kernel(cur_ref, prev_ref, ab_ref, cw_ref, alog_ref, dtb_ref,
                        m_ref, l2_ref, vb_ref, kbd_ref, qd_ref, cd_ref, xc_ref):
    blk = pl.program_id(1)
    xc_ref[0:8, :] = jnp.where(blk > 0, prev_ref[...], 0.0)
    xc_ref[8:8 + TM, :] = cur_ref[...]
    w = cw_ref[...]

    def conv(cols):
        y = xc_ref[8:8 + TM, cols] * w[CONVW - 1:CONVW, cols]
        for j in range(CONVW - 1):
            y = y + xc_ref[5 + j:5 + j + TM, cols] * w[j:j + 1, cols]
        return _silu(y)

    def get_qkv(h):
        return tuple(conv(slice(p * QKW + h * DH, p * QKW + (h + 1) * DH)) for p in range(3))

    _delta_prep(get_qkv, ab_ref[...], alog_ref[...], dtb_ref[...], None, CHUNK, CHUNK,
                m_ref, l2_ref, vb_ref, kbd_ref, qd_ref, cd_ref)


def _prep_prompt(proj1, proj_ab, conv_w, alog, dtb):
    cpb = TM // CHUNK * NH
    nch = RP // CHUNK * NH
    row = lambda b, k: b * TPB + k
    tok = pl.BlockSpec((TM, QKW), lambda b, k: (row(b, k), 0))
    return pl.pallas_call(
        _prep_prompt_kernel,
        grid=(BP, TPB),
        in_specs=[pl.BlockSpec((TM, CONVC), lambda b, k: (row(b, k), 0)),
                  pl.BlockSpec((8, CONVC), lambda b, k: (jnp.maximum(row(b, k) * (TM // 8) - 1, 0), 0)),
                  pl.BlockSpec((TM, 256), lambda b, k: (row(b, k), 0)),
                  pl.BlockSpec((CONVW, CONVC), lambda b, k: (0, 0)),
                  pl.BlockSpec((1, 128), lambda b, k: (0, 0)),
                  pl.BlockSpec((1, 128), lambda b, k: (0, 0))],
        out_specs=[pl.BlockSpec((cpb, CHUNK, CHUNK), lambda b, k: (row(b, k), 0, 0)),
                   pl.BlockSpec((cpb, CHUNK + DH, CHUNK), lambda b, k: (row(b, k), 0, 0)),
                   tok, tok, tok,
                   pl.BlockSpec((TM, 128), lambda b, k: (row(b, k), 0))],
        out_shape=[jax.ShapeDtypeStruct((nch, CHUNK, CHUNK), F32),
                   jax.ShapeDtypeStruct((nch, CHUNK + DH, CHUNK), BF16),
                   jax.ShapeDtypeStruct((RP, QKW), BF16),
                   jax.ShapeDtypeStruct((RP, QKW), BF16),
                   jax.ShapeDtypeStruct((RP, QKW), BF16),
                   jax.ShapeDtypeStruct((RP, 128), F32)],
        scratch_shapes=[pltpu.VMEM((TM + 8, CONVC), F32)],
        compiler_params=_params("arbitrary", "arbitrary"),
        name="delta_prep_prompt",
    )(proj1, proj1, proj_ab, conv_w, alog, dtb)


def _prep_sample_kernel(xp_ref, ab_ref, cw_ref, alog_ref, dtb_ref,
                        m_ref, l2_ref, vb_ref, kbd_ref, qd_ref, cd_ref):
    nrow = SSEQ * SPAD
    w = cw_ref[...]
    valid = (lax.broadcasted_iota(jnp.int32, (nrow, 1), 0) & (SPAD - 1)) < TS

    def conv(cols):
        x = xp_ref[:, cols]
        y = x * w[0:1, cols]
        for j in range(1, CONVW):
            y = y + pltpu.roll(x, nrow - j, axis=0) * w[j:j + 1, cols]
        return jnp.where(valid, _silu(y), 0.0)

    def get_qkv(h):
        return tuple(conv(slice(p * QKW + h * DH, p * QKW + (h + 1) * DH)) for p in range(3))

    _delta_prep(get_qkv, ab_ref[...], alog_ref[...], dtb_ref[...], valid, nrow, SPAD,
                m_ref, l2_ref, vb_ref, kbd_ref, qd_ref, cd_ref)


def _prep_sample(xp8, ab8, conv_w, alog, dtb):
    nrow = SSEQ * SPAD
    steps = BS // SSEQ
    tok = pl.BlockSpec((nrow, QKW), lambda s: (s, 0))
    mat = pl.BlockSpec((NH, nrow, nrow), lambda s: (s, 0, 0))
    return pl.pallas_call(
        _prep_sample_kernel,
        grid=(steps,),
        in_specs=[pl.BlockSpec((nrow, CONVC), lambda s: (s, 0)),
                  pl.BlockSpec((nrow, 256), lambda s: (s, 0)),
                  pl.BlockSpec((CONVW, CONVC), lambda s: (0, 0)),
                  pl.BlockSpec((1, 128), lambda s: (0, 0)),
                  pl.BlockSpec((1, 128), lambda s: (0, 0))],
        out_specs=[mat, pl.BlockSpec((NH, nrow + DH, nrow), lambda s: (s, 0, 0)),
                   tok, tok, tok, pl.BlockSpec((nrow, 128), lambda s: (s, 0))],
        out_shape=[jax.ShapeDtypeStruct((steps * NH, nrow, nrow), F32),
                   jax.ShapeDtypeStruct((steps * NH, nrow + DH, nrow), BF16),
                   jax.ShapeDtypeStruct((BS * SPAD, QKW), BF16),
                   jax.ShapeDtypeStruct((BS * SPAD, QKW), BF16),
                   jax.ShapeDtypeStruct((BS * SPAD, QKW), BF16),
                   jax.ShapeDtypeStruct((BS * SPAD, 128), F32)],
        compiler_params=_params("arbitrary"),
        name="delta_prep_sample",
    )(xp8, ab8, conv_w, alog, dtb)


def _solve_kernel(m_ref, x_ref):
    n, _, lanes = m_ref.shape
    rowid = lax.broadcasted_iota(jnp.int32, (n, lanes), 0)

    def body_i(i, carry):
        def body_j(j, acc):
            return acc - m_ref[i, pl.ds(j, 1), :] * x_ref[j]

        x_ref[i] = lax.fori_loop(0, i, body_j, (rowid == i).astype(F32))
        return carry

    lax.fori_loop(0, n, body_i, 0)


def _solve(m, name):
    n, _, g = m.shape
    lanes = 128
    spec = pl.BlockSpec((n, n, lanes), lambda s: (0, 0, s))
    return pl.pallas_call(
        _solve_kernel,
        grid=(g // lanes,),
        in_specs=[spec],
        out_specs=spec,
        out_shape=jax.ShapeDtypeStruct((n, n, g), F32),
        compiler_params=_params("arbitrary"),
        name=name,
    )(m)


CPS = 2


def _scan_prompt_kernel(t_ref, l2_ref, vb_ref, kbd_ref, qd_ref, cd_ref, o_ref, sout_ref, s_ref, ub_ref, wd_ref):
    step = pl.program_id(0)

    @pl.when(step == 0)
    def _():
        s_ref[...] = jnp.zeros_like(s_ref)

    chains = [(b, h) for b in range(BP) for h in range(NH)]
    for c in range(CPS):
        rows = slice(c * CHUNK, (c + 1) * CHUNK)
        for n, (b, h) in enumerate(chains):
            cols = slice(h * DH, (h + 1) * DH)
            rhs = jnp.concatenate([vb_ref[b, rows, cols], kbd_ref[b, rows, cols]], axis=1)
            r1 = _dot(t_ref[b, c * NH + h], rhs)
            ub_ref[c * len(chains) + n] = r1[:, :DH]
            wd_ref[c * len(chains) + n] = r1[:, DH:].astype(BF16)
    for c in range(CPS):
        rows = slice(c * CHUNK, (c + 1) * CHUNK)
        r2s = []
        for n, (b, h) in enumerate(chains):
            cols = slice(h * DH, (h + 1) * DH)
            lhs = jnp.concatenate([wd_ref[c * len(chains) + n], qd_ref[b, rows, cols]], axis=0)
            r2s.append(_dot(lhs, s_ref[n].astype(BF16)))
        for n, (b, h) in enumerate(chains):
            cols = slice(h * DH, (h + 1) * DH)
            r2 = r2s[n]
            u = (ub_ref[c * len(chains) + n] - r2[:CHUNK]).astype(BF16)
            r3 = _dot(l2_ref[b, c * NH + h], u)
            o_ref[b, rows, cols] = r2[CHUNK:] + r3[:CHUNK]
            cd = cd_ref[b, c * CHUNK:c * CHUNK + 1, h:h + 1]
            s_ref[n] = s_ref[n] * cd + r3[CHUNK:]

    @pl.when(step == pl.num_programs(0) - 1)
    def _():
        sout_ref[...] = s_ref[...]


def _scan_prompt(t, l2, vb, kbd, qd, cd):
    nchunk = TP // CHUNK
    rows = CPS * CHUNK
    t = t.reshape(BP, nchunk * NH, CHUNK, CHUNK)
    l2 = l2.reshape(BP, nchunk * NH, CHUNK + DH, CHUNK)
    tok3 = lambda a: a.reshape(BP, TP, a.shape[-1])
    tok = pl.BlockSpec((BP, rows, QKW), lambda s: (0, s, 0))
    o, s_fin = pl.pallas_call(
        _scan_prompt_kernel,
        grid=(nchunk // CPS,),
        in_specs=[pl.BlockSpec((BP, CPS * NH, CHUNK, CHUNK), lambda s: (0, s, 0, 0)),
                  pl.BlockSpec((BP, CPS * NH, CHUNK + DH, CHUNK), lambda s: (0, s, 0, 0)),
                  tok, tok, tok,
                  pl.BlockSpec((BP, rows, 128), lambda s: (0, s, 0))],
        out_specs=[tok, pl.BlockSpec((BP * NH, DH, DH), lambda s: (0, 0, 0))],
        out_shape=[jax.ShapeDtypeStruct((BP, TP, QKW), F32),
                   jax.ShapeDtypeStruct((BP * NH, DH, DH), F32)],
        scratch_shapes=[pltpu.VMEM((BP * NH, DH, DH), F32),
                        pltpu.VMEM((CPS * BP * NH, CHUNK, DH), F32),
                        pltpu.VMEM((CPS * BP * NH, CHUNK, DH), BF16)],
        compiler_params=_params("arbitrary"),
        name="delta_scan_prompt",
    )(t, l2, tok3(vb), tok3(kbd), tok3(qd), tok3(cd))
    return o.reshape(RP, QKW), s_fin.reshape(BP, NH, DH, DH)


def _scan_sample_kernel(t_ref, l2_ref, vb_ref, kbd_ref, qd_ref, cd_ref, s0_ref, o_ref, sout_ref):
    nrow = SSEQ * SPAD
    seq_of_row = lax.broadcasted_iota(jnp.int32, (nrow, 1), 0) >> (SPAD.bit_length() - 1)
    for h in range(NH):
        cols = slice(h * DH, (h + 1) * DH)
        rhs = jnp.concatenate([vb_ref[:, cols], kbd_ref[:, cols]], axis=1)
        r1 = _dot(t_ref[h], rhs)
        qd = qd_ref[:, cols].astype(F32)
        u_parts, qs_parts = [], []
        for b in range(SSEQ):
            rows = slice(b * SPAD, (b + 1) * SPAD)
            lhs = jnp.concatenate([r1[rows, DH:], qd[rows]], axis=0).astype(BF16)
            r2 = _dot(lhs, s0_ref[b, h].astype(BF16))
            u_parts.append(r1[rows, :DH] - r2[:SPAD])
            qs_parts.append(r2[SPAD:])
        u = jnp.concatenate(u_parts, axis=0)
        o_ref[:, cols] = jnp.concatenate(qs_parts, axis=0) + _dot(l2_ref[h, :nrow, :], u.astype(BF16))
        ktt = l2_ref[h, nrow:, :]
        for b in range(SSEQ):
            ub = jnp.where(seq_of_row == b, u, 0.0).astype(BF16)
            cd = cd_ref[b * SPAD:b * SPAD + 1, h:h + 1]
            sout_ref[b, h] = s0_ref[b, h] * cd + _dot(ktt, ub)


def _scan_sample(t, l2, vb, kbd, qd, cd, s0):
    nrow = SSEQ * SPAD
    steps = BS // SSEQ
    tok = pl.BlockSpec((nrow, QKW), lambda s: (s, 0))
    st = pl.BlockSpec((SSEQ, NH, DH, DH), lambda s: (s, 0, 0, 0))
    return pl.pallas_call(
        _scan_sample_kernel,
        grid=(steps,),
        in_specs=[pl.BlockSpec((NH, nrow, nrow), lambda s: (s, 0, 0)),
                  pl.BlockSpec((NH, nrow + DH, nrow), lambda s: (s, 0, 0)),
                  tok, tok, tok, pl.BlockSpec((nrow, 128), lambda s: (s, 0)), st],
        out_specs=[tok, st],
        out_shape=[jax.ShapeDtypeStruct((BS * SPAD, QKW), F32),
                   jax.ShapeDtypeStruct((BS, NH, DH, DH), F32)],
        compiler_params=_params("arbitrary"),
        name="delta_scan_sample",
    )(t, l2, vb, kbd, qd, cd, s0)


TO = 256
NPO = RP // TO
TPO = TP // TO
POOL_PREV = 16


def _resident(shape):
    return pl.BlockSpec(shape, lambda i: (0,) * len(shape), pipeline_mode=pl.Buffered(1))


def _branch_merge_kernel(op_ref, os_ref, z_ref, x_ref, prev_ref, xs_ref, hist_ref, ga_ref, gb_ref,
                         og_ref, wa_ref, pw_ref, ps_ref, wb_ref, m_ref, xc_ref, yp_ref):
    i = pl.program_id(0)

    o = jnp.where(i < NPO, op_ref[...], os_ref[...])
    og = og_ref[...]
    parts = []
    for h in range(NH):
        cols = slice(h * DH, (h + 1) * DH)
        parts.append((_rms(o[:, cols], og) * _silu(z_ref[:, cols])).astype(BF16))
    ya = _dot(jnp.concatenate(parts, axis=1), wa_ref[...])

    @pl.when(i < NPO)
    def _():
        k = i % TPO
        xc_ref[0:POOL_PREV, :] = jnp.where(k > 0, prev_ref[...], 0.0)
        xc_ref[POOL_PREV:POOL_PREV + TO, :] = x_ref[...]
        pos = k * TO + lax.broadcasted_iota(jnp.int32, (TO, 1), 0)
        for gi, win in enumerate(POOL_WINDOWS):
            cols = slice(gi * PGRP, (gi + 1) * PGRP)
            acc = x_ref[:, cols]
            for d in range(1, win):
                acc = acc + xc_ref[POOL_PREV - d:POOL_PREV - d + TO, cols]
            cnt = jnp.minimum(win, pos + 1).astype(F32)
            yp_ref[:, cols] = acc / cnt - x_ref[:, cols]

    for half in range(RS // TO):
        @pl.when(i == NPO + half)
        def _(half=half):
            for gi, win in enumerate(POOL_WINDOWS):
                cols = slice(gi * PGRP, (gi + 1) * PGRP)
                for tl in range(TO // BS):
                    t = half * (TO // BS) + tl
                    acc = xs_ref[t * BS:(t + 1) * BS, cols]
                    for d in range(1, win):
                        src = PHIST + t - d
                        if src >= PHIST:
                            acc = acc + xs_ref[(src - PHIST) * BS:(src - PHIST + 1) * BS, cols]
                        else:
                            acc = acc + hist_ref[src * BS:(src + 1) * BS, cols]
                    yp_ref[tl * BS:(tl + 1) * BS, cols] = acc / float(win) - xs_ref[t * BS:(t + 1) * BS, cols]

    parts = []
    for gi in range(len(POOL_WINDOWS)):
        cols = slice(gi * PGRP, (gi + 1) * PGRP)
        parts.append(_dot(yp_ref[:, cols].astype(BF16), pw_ref[gi]))
    yp = jnp.concatenate(parts, axis=1) * ps_ref[...]
    yb = _dot(yp.astype(BF16), wb_ref[...])

    m_ref[...] = (_sigmoid(ga_ref[...]) * ya + _sigmoid(gb_ref[...]) * yb).astype(BF16)


def _branch_merge(o_p, o_s, proj1, proj3, hist_s, o_norm_g, w_proj_a, pool_w, pool_scale, w_proj_b):
    xcol = 2 * D // PW
    return pl.pallas_call(
        _branch_merge_kernel,
        grid=(R // TO,),
        in_specs=[pl.BlockSpec((TO, QKW), lambda i: (jnp.minimum(i, NPO - 1), 0)),
                  pl.BlockSpec((TO, QKW), lambda i: (jnp.maximum(i - NPO, 0), 0)),
                  pl.BlockSpec((TO, QKW), lambda i: (i, 3)),
                  pl.BlockSpec((TO, PW), lambda i: (i, xcol)),
                  pl.BlockSpec((POOL_PREV, PW), lambda i: (jnp.maximum(i * (TO // POOL_PREV) - 1, 0), xcol)),
                  pl.BlockSpec((RS, PW), lambda i: (RP // RS, xcol), pipeline_mode=pl.Buffered(1)),
                  _resident((PHIST * BS, PW)),
                  pl.BlockSpec((TO, D), lambda i: (i, 0)),
                  pl.BlockSpec((TO, D), lambda i: (i, 1)),
                  _resident((1, DH)),
                  _resident((QKW, D)),
                  _resident((len(POOL_WINDOWS), PGRP, PGRP)),
                  _resident((1, PW)),
                  _resident((PW, D))],
        out_specs=pl.BlockSpec((TO, D), lambda i: (i, 0)),
        out_shape=jax.ShapeDtypeStruct((R, D), BF16),
        scratch_shapes=[pltpu.VMEM((TO + POOL_PREV, PW), F32), pltpu.VMEM((TO, PW), F32)],
        compiler_params=_params("arbitrary"),
        name="branch_merge",
    )(o_p, o_s, proj1, proj3, proj3, proj3, hist_s, proj3, proj3, o_norm_g, w_proj_a, pool_w, pool_scale, w_proj_b)


def _post_out_kernel(m_ref, xp_ref, xs_ref,
                     gtp_ref, shp_ref, scp_ref, gts_ref, shs_ref, scs_ref, g2_ref, w_ref,
                     x1_ref, h_ref):
    i = pl.program_id(0)
    mix = _dot(m_ref[...], w_ref[...])
    g2 = g2_ref[...]

    @pl.when(i < NPO)
    def _():
        x1 = xp_ref[...] + gtp_ref[0] * mix
        x1_ref[...] = x1
        h_ref[...] = (_rms(x1, g2) * (1.0 + scp_ref[0]) + shp_ref[0]).astype(BF16)

    @pl.when(i >= NPO)
    def _():
        for t in range(TO // BS):
            rows = slice(t * BS, (t + 1) * BS)
            x1 = xs_ref[rows, :] + gts_ref[...] * mix[rows]
            x1_ref[rows, :] = x1
            h_ref[rows, :] = (_rms(x1, g2) * (1.0 + scs_ref[...]) + shs_ref[...]).astype(BF16)


def _post_out(merged, xp2, xs2, mod_p, mod_s, norm2_g, w_out):
    gtp, gts = _mod_specs(2, TPO)
    shp, shs = _mod_specs(3, TPO)
    scp, scs = _mod_specs(4, TPO)
    tile = pl.BlockSpec((TO, D), lambda i: (i, 0))
    return pl.pallas_call(
        _post_out_kernel,
        grid=(R // TO,),
        in_specs=[tile,
                  pl.BlockSpec((TO, D), lambda i: (jnp.minimum(i, NPO - 1), 0)),
                  pl.BlockSpec((TO, D), lambda i: (jnp.maximum(i - NPO, 0), 0)),
                  gtp, shp, scp, gts, shs, scs,
                  _resident((1, D)),
                  _resident((D, D))],
        out_specs=[tile, tile],
        out_shape=[jax.ShapeDtypeStruct((R, D), F32), jax.ShapeDtypeStruct((R, D), BF16)],
        compiler_params=_params("arbitrary"),
        name="out_proj_norm2",
    )(merged, xp2, xs2, mod_p, mod_p, mod_p, mod_s, mod_s, mod_s, norm2_g, w_out)


FF_TN = 512
FF_TK = 1408


def _ffn1_kernel(h_ref, wg_ref, wu_ref, a_ref, wgb_ref, wub_ref):
    @pl.when(pl.program_id(1) == 0)
    def _():
        wgb_ref[...] = wg_ref[...].astype(BF16)
        wub_ref[...] = wu_ref[...].astype(BF16)

    h = h_ref[...]
    a_ref[...] = (_silu(_dot(h, wgb_ref[...])) * _dot(h, wub_ref[...])).astype(BF16)


def _ffn1(h, w_gate_up):
    nj = DFF // FF_TN
    return pl.pallas_call(
        _ffn1_kernel,
        grid=(nj, R // TMM),
        in_specs=[pl.BlockSpec((TMM, D), lambda j, i: (i, 0)),
                  pl.BlockSpec((D, FF_TN), lambda j, i: (0, j)),
                  pl.BlockSpec((D, FF_TN), lambda j, i: (0, nj + j))],
        out_specs=pl.BlockSpec((TMM, FF_TN), lambda j, i: (i, j)),
        out_shape=jax.ShapeDtypeStruct((R, DFF), BF16),
        scratch_shapes=[pltpu.VMEM((D, FF_TN), BF16), pltpu.VMEM((D, FF_TN), BF16)],
        compiler_params=_params("arbitrary", "arbitrary"),
        name="ffn_gate_up",
    )(h, w_gate_up, w_gate_up)


def _ffn2_kernel(a_ref, w_ref, x1_ref, gtp_ref, gts_ref, fg_ref, yp_ref, ys_ref, acc_ref):
    i = pl.program_id(0)
    k = pl.program_id(1)
    nk = DFF // FF_TK

    @pl.when(k == 0)
    def _():
        acc_ref[...] = jnp.zeros_like(acc_ref)

    acc_ref[...] += _dot(a_ref[...], w_ref[...])
    fg = fg_ref[...]

    @pl.when((k == nk - 1) & (i < NPT))
    def _():
        yp_ref[...] = _rms(x1_ref[...] + gtp_ref[0] * acc_ref[...], fg)

    @pl.when((k == nk - 1) & (i == NPT))
    def _():
        for t in range(TS):
            rows = slice(t * BS, (t + 1) * BS)
            ys_ref[rows, :] = _rms(x1_ref[rows, :] + gts_ref[...] * acc_ref[rows, :], fg)


def _ffn2(act, w_down, x1, mod_p, mod_s, final_g):
    gtp, gts = _mod_specs(5, TPB)
    return pl.pallas_call(
        _ffn2_kernel,
        grid=(R // TM, DFF // FF_TK),
        in_specs=[pl.BlockSpec((TM, FF_TK), lambda i, k: (i, k)),
                  pl.BlockSpec((FF_TK, D), lambda i, k: (k, 0)),
                  pl.BlockSpec((TM, D), lambda i, k: (i, 0)),
                  gtp, gts,
                  pl.BlockSpec((1, D), lambda i, k: (0, 0))],
        out_specs=[pl.BlockSpec((TM, D), lambda i, k: (jnp.minimum(i, NPT - 1), 0)),
                   pl.BlockSpec((TM, D), lambda i, k: (0, 0))],
        out_shape=[jax.ShapeDtypeStruct((RP, D), F32), jax.ShapeDtypeStruct((RS, D), F32)],
        scratch_shapes=[pltpu.VMEM((TM, D), F32)],
        compiler_params=_params("arbitrary", "arbitrary"),
        name="ffn_down_final",
    )(act, w_down, x1, mod_p, mod_s, final_g)


def _to_time_major(x):
    return jnp.transpose(x, (1, 0, 2)).reshape(TS * BS, x.shape[-1])


def _to_batch_major(x):
    return jnp.transpose(x.reshape(TS, BS, x.shape[-1]), (1, 0, 2))


def _pad_lanes(v):
    return jnp.zeros((1, 128), F32).at[0, :NH].set(v.astype(F32))


def kernel(x_prompt, x_sample, c_prompt, c_sample, state_delta, state_conv, state_pool, w_ada, b_ada, norm1_g,
           w_in, conv_w, a_log, dt_bias, o_norm_g, pool_w, pool_scale, w_proj_a, w_proj_b, w_out, norm2_g,
           w_gate_up, w_down, final_g):
    assert w_ada.shape[0] == 1, "single layer"
    xp2 = x_prompt.reshape(RP, D)
    xs2 = _to_time_major(x_sample)

    c_all = jnp.concatenate([c_sample, c_prompt, jnp.zeros((4, D), F32)], axis=0)
    mod = _mod(c_all, w_ada[0], b_ada[0].reshape(1, 6 * D))
    mod_s = mod[:BS]
    mod_p = mod[BS:BS + BP].reshape(BP, 1, 6 * D)

    u = _pre(xp2, xs2, mod_p, mod_s, norm1_g[0].reshape(1, D))

    w_in0 = w_in[0]
    o_ab = 4 * QKW
    w_ab = jnp.zeros((D, 256), F32)
    w_ab = w_ab.at[:, :NH].set(w_in0[:, o_ab:o_ab + NH]).at[:, 128:128 + NH].set(w_in0[:, o_ab + NH:o_ab + 2 * NH])
    proj1 = _matmul(u, w_in0, 4 * QKW, 1024, "in_proj_qkvz")
    proj_ab = _matmul(u, w_ab, 256, 256, "in_proj_ab")
    proj3 = _matmul_shifted(u, w_in0, o_ab, 2 * NH, (PW + 2 * D) // 1024, PW // 1024, 1024, "in_proj_gates_pool")

    cw = conv_w[0]
    alog = _pad_lanes(a_log[0])
    dtb = _pad_lanes(dt_bias[0])

    m_p, l2_p, vb_p, kbd_p, qd_p, cd_p = _prep_prompt(proj1, proj_ab, cw, alog, dtb)
    t_p = _solve(jnp.transpose(m_p, (1, 2, 0)), "tri_solve_prompt")
    t_p = jnp.transpose(t_p, (2, 0, 1)).astype(BF16)
    o_p, s_p = _scan_prompt(t_p, l2_p, vb_p, kbd_p, qd_p, cd_p)

    qkv_s = _to_batch_major(proj1[RP:, :CONVC])
    conv_in = jnp.concatenate([state_conv[0], qkv_s, jnp.zeros((BS, SPAD - TS - (CONVW - 1), CONVC), F32)], axis=1)
    ab_s = _to_batch_major(proj_ab[RP:])
    ab_s = jnp.concatenate([ab_s, jnp.zeros((BS, SPAD - TS, 256), F32)], axis=1)
    m_s, l2_s, vb_s, kbd_s, qd_s, cd_s = _prep_sample(
        conv_in.reshape(BS * SPAD, CONVC), ab_s.reshape(BS * SPAD, 256), cw, alog, dtb)
    nblk = BS // SSEQ
    m_blocks = jnp.diagonal(m_s.reshape(nblk * NH, SSEQ, SPAD, SSEQ, SPAD), axis1=1, axis2=3)
    m_blocks = jnp.transpose(m_blocks, (1, 2, 0, 3)).reshape(SPAD, SPAD, nblk * NH * SSEQ)
    t_blocks = _solve(m_blocks, "tri_solve_sample").reshape(SPAD, SPAD, nblk * NH, SSEQ)
    t_s = jnp.einsum("ijgb,bc->gbicj", t_blocks, jnp.eye(SSEQ, dtype=F32))
    t_s = t_s.reshape(nblk * NH, SSEQ * SPAD, SSEQ * SPAD).astype(BF16)
    o_s8, s_s = _scan_sample(t_s, l2_s, vb_s, kbd_s, qd_s, cd_s, state_delta[0])
    o_s = _to_time_major(o_s8.reshape(BS, SPAD, QKW)[:, :TS])

    hist_s = jnp.transpose(state_pool[0], (1, 0, 2)).reshape(PHIST * BS, PW)
    merged = _branch_merge(o_p, o_s, proj1, proj3, hist_s, o_norm_g[0].reshape(1, DH), w_proj_a[0].astype(BF16),
                           pool_w[0].astype(BF16), pool_scale[0].reshape(1, PW), w_proj_b[0].astype(BF16))
    x1, h = _post_out(merged, xp2, xs2, mod_p, mod_s, norm2_g[0].reshape(1, D), w_out[0].astype(BF16))
    act = _ffn1(h, w_gate_up[0])
    y_p, y_s = _ffn2(act, w_down[0].astype(BF16), x1, mod_p, mod_s, final_g.reshape(1, D))

    xpool_s = _to_batch_major(proj3[RP:, 2 * D:])

    def tail_rows(a, n, c0, c1):
        return jnp.stack([a[(b + 1) * TP - n:(b + 1) * TP, c0:c1] for b in range(BP)])

    return (y_p.reshape(BP, TP, D),
            _to_batch_major(y_s),
            s_p[None],
            tail_rows(proj1, CONVW - 1, 0, CONVC)[None],
            tail_rows(proj3, PHIST, 2 * D, 2 * D + PW)[None],
            s_s[None],
            conv_in[:, TS:TS + CONVW - 1][None],
            jnp.concatenate([state_pool[0][:, TS:], xpool_s], axis=1)[None])
```

```python
import functools

import jax
import jax.numpy as jnp
from jax import lax
from jax.experimental import pallas as pl
from jax.experimental.pallas import tpu as pltpu

F32 = jnp.float32
BF16 = jnp.bfloat16

D = 2048
NH = 8
DH = 128
QKW = NH * DH
CONVC = 3 * QKW
CONVW = 4
PW = 1024
PGRP = 256
POOL_WINDOWS = (2, 4, 8, 16)
PHIST = 15
DFF = 5632
EPS = 1e-6

BP, TP = 4, 2048
BS, TS = 128, 4
RP = BP * TP
RS = BS * TS
R = RP + RS
TM = 512
NPT = RP // TM
TPB = TP // TM
TMM = R // 8
CHUNK = 64
SPAD = 8
SSEQ = 16

VMEM_LIMIT = 56 * 1024 * 1024

NT_DIMS = (((1,), (1,)), ((), ()))


def _params(*sem):
    return pltpu.CompilerParams(dimension_semantics=sem, vmem_limit_bytes=VMEM_LIMIT)


def _sigmoid(x):
    return 1.0 / (1.0 + jnp.exp(-x))


def _silu(x):
    return x * _sigmoid(x)


def _softplus(x):
    return jnp.maximum(x, 0.0) + jnp.log1p(jnp.exp(-jnp.abs(x)))


def _rms(x, gain):
    return x * lax.rsqrt(jnp.mean(x * x, axis=-1, keepdims=True) + EPS) * gain


def _dot(a, b):
    return jnp.dot(a, b, preferred_element_type=F32)


def _mod_kernel(c_ref, w_ref, b_ref, o_ref):
    c = c_ref[...]
    o_ref[...] = _dot(_silu(c).astype(BF16), w_ref[...].astype(BF16)) + b_ref[...]


def _mod(c_all, w_ada, b_ada):
    n = c_all.shape[0]
    tn = 1024
    return pl.pallas_call(
        _mod_kernel,
        grid=(6 * D // tn,),
        in_specs=[pl.BlockSpec((n, D), lambda j: (0, 0)),
                  pl.BlockSpec((D, tn), lambda j: (0, j)),
                  pl.BlockSpec((1, tn), lambda j: (0, j))],
        out_specs=pl.BlockSpec((n, tn), lambda j: (0, j)),
        out_shape=jax.ShapeDtypeStruct((n, 6 * D), F32),
        compiler_params=_params("arbitrary"),
        name="ada_mod",
    )(c_all, w_ada, b_ada)


def _mod_specs(col, tiles_per_seq):
    p = pl.BlockSpec((1, 1, D), lambda i, *_: (jnp.minimum(i // tiles_per_seq, BP - 1), 0, col))
    s = pl.BlockSpec((BS, D), lambda i, *_: (0, col))
    return p, s


def _pre_kernel(xp_ref, xs_ref, shp_ref, scp_ref, shs_ref, scs_ref, g_ref, u_ref):
    i = pl.program_id(0)
    g = g_ref[...]

    @pl.when(i < NPT)
    def _():
        u = _rms(xp_ref[...], g) * (1.0 + scp_ref[0]) + shp_ref[0]
        u_ref[...] = u.astype(BF16)

    @pl.when(i == NPT)
    def _():
        for t in range(TS):
            rows = slice(t * BS, (t + 1) * BS)
            u = _rms(xs_ref[rows, :], g) * (1.0 + scs_ref[...]) + shs_ref[...]
            u_ref[rows, :] = u.astype(BF16)


def _pre(xp2, xs2, mod_p, mod_s, g):
    shp, shs = _mod_specs(0, TPB)
    scp, scs = _mod_specs(1, TPB)
    return pl.pallas_call(
        _pre_kernel,
        grid=(NPT + 1,),
        in_specs=[pl.BlockSpec((TM, D), lambda i: (jnp.minimum(i, NPT - 1), 0)),
                  pl.BlockSpec((TM, D), lambda i: (0, 0)),
                  shp, scp, shs, scs,
                  pl.BlockSpec((1, D), lambda i: (0, 0))],
        out_specs=pl.BlockSpec((TM, D), lambda i: (i, 0)),
        out_shape=jax.ShapeDtypeStruct((R, D), BF16),
        compiler_params=_params("arbitrary"),
        name="norm1_mod",
    )(xp2, xs2, mod_p, mod_p, mod_s, mod_s, g)


def _mm_kernel(a_ref, w_ref, o_ref, wb_ref):
    @pl.when(pl.program_id(1) == 0)
    def _():
        wb_ref[...] = w_ref[...].astype(BF16)

    o_ref[...] = _dot(a_ref[...], wb_ref[...])


def _matmul(a, w, n_cols, tn, name):
    rows, k = a.shape
    return pl.pallas_call(
        _mm_kernel,
        grid=(n_cols // tn, rows // TMM),
        in_specs=[pl.BlockSpec((TMM, k), lambda j, i: (i, 0)),
                  pl.BlockSpec((k, tn), lambda j, i: (0, j))],
        out_specs=pl.BlockSpec((TMM, tn), lambda j, i: (i, j)),
        out_shape=jax.ShapeDtypeStruct((rows, n_cols), F32),
        scratch_shapes=[pltpu.VMEM((k, tn), BF16)],
        compiler_params=_params("arbitrary", "arbitrary"),
        name=name,
    )(a, w)


def _mm_shift_kernel(shift, a_ref, w0_ref, w1_ref, o_ref, wb_ref):
    @pl.when(pl.program_id(1) == 0)
    def _():
        w = jnp.concatenate([w0_ref[:, shift:], w1_ref[:, :shift]], axis=1)
        wb_ref[...] = w.astype(BF16)

    o_ref[...] = _dot(a_ref[...], wb_ref[...])


def _matmul_shifted(a, w, col0, shift, nt, rot, tn, name):
    rows, k = a.shape
    src = lambda j: lax.rem(j + rot, nt)
    return pl.pallas_call(
        functools.partial(_mm_shift_kernel, shift),
        grid=(nt, rows // TMM),
        in_specs=[pl.BlockSpec((TMM, k), lambda j, i: (i, 0)),
                  pl.BlockSpec((k, tn), lambda j, i: (0, col0 // tn + src(j))),
                  pl.BlockSpec((k, 128), lambda j, i: (0, (col0 + (src(j) + 1) * tn) // 128))],
        out_specs=pl.BlockSpec((TMM, tn), lambda j, i: (i, j)),
        out_shape=jax.ShapeDtypeStruct((rows, nt * tn), F32),
        scratch_shapes=[pltpu.VMEM((k, tn), BF16)],
        compiler_params=_params("arbitrary", "arbitrary"),
        name=name,
    )(a, w, w)


def _delta_prep(get_qkv, ab, alog, dtb, valid, chunk, seq_block, m_ref, l2_ref, vb_ref, kbd_ref, qd_ref, cd_ref):
    rows = ab.shape[0]
    a = ab[:, :128]
    b = ab[:, 128:]
    beta = _sigmoid(b)
    g = -jnp.exp(alog) * _softplus(a + dtb)
    if valid is not None:
        beta = jnp.where(valid, beta, 0.0)
        g = jnp.where(valid, g, 0.0)
    shift = seq_block.bit_length() - 1
    r = lax.broadcasted_iota(jnp.int32, (rows, rows), 0)
    c = lax.broadcasted_iota(jnp.int32, (rows, rows), 1)
    same = (r >> shift) == (c >> shift)
    ltri = jnp.where(same & (r >= c), 1.0, 0.0)
    lall = jnp.where(same, 1.0, 0.0)
    dcol = jnp.dot(ltri, g, preferred_element_type=F32, precision=lax.Precision.HIGHEST)
    last = jnp.dot(lall, g, preferred_element_type=F32, precision=lax.Precision.HIGHEST)
    cd_ref[...] = jnp.exp(last)
    drow = dcol.T
    edc = jnp.exp(dcol)
    etl = jnp.exp(last - dcol)
    rc = lax.broadcasted_iota(jnp.int32, (chunk, chunk), 0)
    cc = lax.broadcasted_iota(jnp.int32, (chunk, chunk), 1)
    same_c = (rc >> shift) == (cc >> shift)
    causal = same_c & (rc >= cc)
    strict = same_c & (rc > cc)
    for h in range(NH):
        q, k, v = get_qkv(h)
        qn = q * lax.rsqrt(jnp.sum(q * q, axis=-1, keepdims=True) + EPS) * (DH ** -0.5)
        kn = k * lax.rsqrt(jnp.sum(k * k, axis=-1, keepdims=True) + EPS)
        bcol = beta[:, h:h + 1]
        kb = kn * bcol
        cols = slice(h * DH, (h + 1) * DH)
        vb_ref[:, cols] = (v * bcol).astype(BF16)
        kbd_ref[:, cols] = (kb * edc[:, h:h + 1]).astype(BF16)
        qd_ref[:, cols] = (qn * edc[:, h:h + 1]).astype(BF16)
        kt = kn * etl[:, h:h + 1]
        for sc in range(rows // chunk):
            rs = slice(sc * chunk, (sc + 1) * chunk)
            diff = dcol[rs, h:h + 1] - drow[h:h + 1, rs]
            lm = jnp.where(causal, jnp.exp(jnp.where(causal, diff, 0.0)), 0.0)
            lhs = jnp.concatenate([kb[rs], qn[rs]], axis=0).astype(BF16)
            s = lax.dot_general(lhs, kn[rs].astype(BF16), NT_DIMS, preferred_element_type=F32)
            idx = sc * NH + h
            m_ref[idx] = jnp.where(strict, s[:chunk] * lm, 0.0)
            l2_ref[idx] = jnp.concatenate([s[chunk:] * lm, kt[rs].T], axis=0).astype(BF16)


def _prep_prompt_kernel(cur_ref, prev_ref, ab_ref, cw_ref, alog_ref, dtb_ref,
                        m_ref, l2_ref, vb_ref, kbd_ref, qd_ref, cd_ref, xc_ref):
    blk = pl.program_id(1)
    xc_ref[0:8, :] = jnp.where(blk > 0, prev_ref[...], 0.0)
    xc_ref[8:8 + TM, :] = cur_ref[...]
    w = cw_ref[...]

    def conv(cols):
        y = xc_ref[8:8 + TM, cols] * w[CONVW - 1:CONVW, cols]
        for j in range(CONVW - 1):
            y = y + xc_ref[5 + j:5 + j + TM, cols] * w[j:j + 1, cols]
        return _silu(y)

    def get_qkv(h):
        return tuple(conv(slice(p * QKW + h * DH, p * QKW + (h + 1) * DH)) for p in range(3))

    _delta_prep(get_qkv, ab_ref[...], alog_ref[...], dtb_ref[...], None, CHUNK, CHUNK,
                m_ref, l2_ref, vb_ref, kbd_ref, qd_ref, cd_ref)


def _prep_prompt(proj1, proj_ab, conv_w, alog, dtb):
    cpb = TM // CHUNK * NH
    nch = RP // CHUNK * NH
    row = lambda b, k: b * TPB + k
    tok = pl.BlockSpec((TM, QKW), lambda b, k: (row(b, k), 0))
    return pl.pallas_call(
        _prep_prompt_kernel,
        grid=(BP, TPB),
        in_specs=[pl.BlockSpec((TM, CONVC), lambda b, k: (row(b, k), 0)),
                  pl.BlockSpec((8, CONVC), lambda b, k: (jnp.maximum(row(b, k) * (TM // 8) - 1, 0), 0)),
                  pl.BlockSpec((TM, 256), lambda b, k: (row(b, k), 0)),
                  pl.BlockSpec((CONVW, CONVC), lambda b, k: (0, 0)),
                  pl.BlockSpec((1, 128), lambda b, k: (0, 0)),
                  pl.BlockSpec((1, 128), lambda b, k: (0, 0))],
        out_specs=[pl.BlockSpec((cpb, CHUNK, CHUNK), lambda b, k: (row(b, k), 0, 0)),
                   pl.BlockSpec((cpb, CHUNK + DH, CHUNK), lambda b, k: (row(b, k), 0, 0)),
                   tok, tok, tok,
                   pl.BlockSpec((TM, 128), lambda b, k: (row(b, k), 0))],
        out_shape=[jax.ShapeDtypeStruct((nch, CHUNK, CHUNK), F32),
                   jax.ShapeDtypeStruct((nch, CHUNK + DH, CHUNK), BF16),
                   jax.ShapeDtypeStruct((RP, QKW), BF16),
                   jax.ShapeDtypeStruct((RP, QKW), BF16),
                   jax.ShapeDtypeStruct((RP, QKW), BF16),
                   jax.ShapeDtypeStruct((RP, 128), F32)],
        scratch_shapes=[pltpu.VMEM((TM + 8, CONVC), F32)],
        compiler_params=_params("arbitrary", "arbitrary"),
        name="delta_prep_prompt",
    )(proj1, proj1, proj_ab, conv_w, alog, dtb)


def _prep_sample_kernel(xp_ref, ab_ref, cw_ref, alog_ref, dtb_ref,
                        m_ref, l2_ref, vb_ref, kbd_ref, qd_ref, cd_ref):
    nrow = SSEQ * SPAD
    w = cw_ref[...]
    valid = (lax.broadcasted_iota(jnp.int32, (nrow, 1), 0) & (SPAD - 1)) < TS

    def conv(cols):
        x = xp_ref[:, cols]
        y = x * w[0:1, cols]
        for j in range(1, CONVW):
            y = y + pltpu.roll(x, nrow - j, axis=0) * w[j:j + 1, cols]
        return jnp.where(valid, _silu(y), 0.0)

    def get_qkv(h):
        return tuple(conv(slice(p * QKW + h * DH, p * QKW + (h + 1) * DH)) for p in range(3))

    _delta_prep(get_qkv, ab_ref[...], alog_ref[...], dtb_ref[...], valid, nrow, SPAD,
                m_ref, l2_ref, vb_ref, kbd_ref, qd_ref, cd_ref)


def _prep_sample(xp8, ab8, conv_w, alog, dtb):
    nrow = SSEQ * SPAD
    steps = BS // SSEQ
    tok = pl.BlockSpec((nrow, QKW), lambda s: (s, 0))
    mat = pl.BlockSpec((NH, nrow, nrow), lambda s: (s, 0, 0))
    return pl.pallas_call(
        _prep_sample_kernel,
        grid=(steps,),
        in_specs=[pl.BlockSpec((nrow, CONVC), lambda s: (s, 0)),
                  pl.BlockSpec((nrow, 256), lambda s: (s, 0)),
                  pl.BlockSpec((CONVW, CONVC), lambda s: (0, 0)),
                  pl.BlockSpec((1, 128), lambda s: (0, 0)),
                  pl.BlockSpec((1, 128), lambda s: (0, 0))],
        out_specs=[mat, pl.BlockSpec((NH, nrow + DH, nrow), lambda s: (s, 0, 0)),
                   tok, tok, tok, pl.BlockSpec((nrow, 128), lambda s: (s, 0))],
        out_shape=[jax.ShapeDtypeStruct((steps * NH, nrow, nrow), F32),
                   jax.ShapeDtypeStruct((steps * NH, nrow + DH, nrow), BF16),
                   jax.ShapeDtypeStruct((BS * SPAD, QKW), BF16),
                   jax.ShapeDtypeStruct((BS * SPAD, QKW), BF16),
                   jax.ShapeDtypeStruct((BS * SPAD, QKW), BF16),
                   jax.ShapeDtypeStruct((BS * SPAD, 128), F32)],
        compiler_params=_params("arbitrary"),
        name="delta_prep_sample",
    )(xp8, ab8, conv_w, alog, dtb)


def _solve_kernel(m_ref, x_ref):
    n, _, lanes = m_ref.shape
    rowid = lax.broadcasted_iota(jnp.int32, (n, lanes), 0)

    def body_i(i, carry):
        def body_j(j, acc):
            return acc - m_ref[i, pl.ds(j, 1), :] * x_ref[j]

        x_ref[i] = lax.fori_loop(0, i, body_j, (rowid == i).astype(F32))
        return carry

    lax.fori_loop(0, n, body_i, 0)


def _solve(m, name):
    n, _, g = m.shape
    lanes = 128
    spec = pl.BlockSpec((n, n, lanes), lambda s: (0, 0, s))
    return pl.pallas_call(
        _solve_kernel,
        grid=(g // lanes,),
        in_specs=[spec],
        out_specs=spec,
        out_shape=jax.ShapeDtypeStruct((n, n, g), F32),
        compiler_params=_params("arbitrary"),
        name=name,
    )(m)


CPS = 2


def _scan_prompt_kernel(t_ref, l2_ref, vb_ref, kbd_ref, qd_ref, cd_ref, o_ref, sout_ref, s_ref, ub_ref, wd_ref):
    step = pl.program_id(0)

    @pl.when(step == 0)
    def _():
        s_ref[...] = jnp.zeros_like(s_ref)

    chains = [(b, h) for b in range(BP) for h in range(NH)]
    for c in range(CPS):
        rows = slice(c * CHUNK, (c + 1) * CHUNK)
        for n, (b, h) in enumerate(chains):
            cols = slice(h * DH, (h + 1) * DH)
            rhs = jnp.concatenate([vb_ref[b, rows, cols], kbd_ref[b, rows, cols]], axis=1)
            r1 = _dot(t_ref[b, c * NH + h], rhs)
            ub_ref[c * len(chains) + n] = r1[:, :DH]
            wd_ref[c * len(chains) + n] = r1[:, DH:].astype(BF16)
    for c in range(CPS):
        rows = slice(c * CHUNK, (c + 1) * CHUNK)
        r2s = []
        for n, (b, h) in enumerate(chains):
            cols = slice(h * DH, (h + 1) * DH)
            lhs = jnp.concatenate([wd_ref[c * len(chains) + n], qd_ref[b, rows, cols]], axis=0)
            r2s.append(_dot(lhs, s_ref[n].astype(BF16)))
        for n, (b, h) in enumerate(chains):
            cols = slice(h * DH, (h + 1) * DH)
            r2 = r2s[n]
            u = (ub_ref[c * len(chains) + n] - r2[:CHUNK]).astype(BF16)
            r3 = _dot(l2_ref[b, c * NH + h], u)
            o_ref[b, rows, cols] = r2[CHUNK:] + r3[:CHUNK]
            cd = cd_ref[b, c * CHUNK:c * CHUNK + 1, h:h + 1]
            s_ref[n] = s_ref[n] * cd + r3[CHUNK:]

    @pl.when(step == pl.num_programs(0) - 1)
    def _():
        sout_ref[...] = s_ref[...]


def _scan_prompt(t, l2, vb, kbd, qd, cd):
    nchunk = TP // CHUNK
    rows = CPS * CHUNK
    t = t.reshape(BP, nchunk * NH, CHUNK, CHUNK)
    l2 = l2.reshape(BP, nchunk * NH, CHUNK + DH, CHUNK)
    tok3 = lambda a: a.reshape(BP, TP, a.shape[-1])
    tok = pl.BlockSpec((BP, rows, QKW), lambda s: (0, s, 0))
    o, s_fin = pl.pallas_call(
        _scan_prompt_kernel,
        grid=(nchunk // CPS,),
        in_specs=[pl.BlockSpec((BP, CPS * NH, CHUNK, CHUNK), lambda s: (0, s, 0, 0)),
                  pl.BlockSpec((BP, CPS * NH, CHUNK + DH, CHUNK), lambda s: (0, s, 0, 0)),
                  tok, tok, tok,
                  pl.BlockSpec((BP, rows, 128), lambda s: (0, s, 0))],
        out_specs=[tok, pl.BlockSpec((BP * NH, DH, DH), lambda s: (0, 0, 0))],
        out_shape=[jax.ShapeDtypeStruct((BP, TP, QKW), F32),
                   jax.ShapeDtypeStruct((BP * NH, DH, DH), F32)],
        scratch_shapes=[pltpu.VMEM((BP * NH, DH, DH), F32),
                        pltpu.VMEM((CPS * BP * NH, CHUNK, DH), F32),
                        pltpu.VMEM((CPS * BP * NH, CHUNK, DH), BF16)],
        compiler_params=_params("arbitrary"),
        name="delta_scan_prompt",
    )(t, l2, tok3(vb), tok3(kbd), tok3(qd), tok3(cd))
    return o.reshape(RP, QKW), s_fin.reshape(BP, NH, DH, DH)


def _scan_sample_kernel(t_ref, l2_ref, vb_ref, kbd_ref, qd_ref, cd_ref, s0_ref, o_ref, sout_ref):
    nrow = SSEQ * SPAD
    seq_of_row = lax.broadcasted_iota(jnp.int32, (nrow, 1), 0) >> (SPAD.bit_length() - 1)
    for h in range(NH):
        cols = slice(h * DH, (h + 1) * DH)
        rhs = jnp.concatenate([vb_ref[:, cols], kbd_ref[:, cols]], axis=1)
        r1 = _dot(t_ref[h], rhs)
        qd = qd_ref[:, cols].astype(F32)
        u_parts, qs_parts = [], []
        for b in range(SSEQ):
            rows = slice(b * SPAD, (b + 1) * SPAD)
            lhs = jnp.concatenate([r1[rows, DH:], qd[rows]], axis=0).astype(BF16)
            r2 = _dot(lhs, s0_ref[b, h].astype(BF16))
            u_parts.append(r1[rows, :DH] - r2[:SPAD])
            qs_parts.append(r2[SPAD:])
        u = jnp.concatenate(u_parts, axis=0)
        o_ref[:, cols] = jnp.concatenate(qs_parts, axis=0) + _dot(l2_ref[h, :nrow, :], u.astype(BF16))
        ktt = l2_ref[h, nrow:, :]
        for b in range(SSEQ):
            ub = jnp.where(seq_of_row == b, u, 0.0).astype(BF16)
            cd = cd_ref[b * SPAD:b * SPAD + 1, h:h + 1]
            sout_ref[b, h] = s0_ref[b, h] * cd + _dot(ktt, ub)


def _scan_sample(t, l2, vb, kbd, qd, cd, s0):
    nrow = SSEQ * SPAD
    steps = BS // SSEQ
    tok = pl.BlockSpec((nrow, QKW), lambda s: (s, 0))
    st = pl.BlockSpec((SSEQ, NH, DH, DH), lambda s: (s, 0, 0, 0))
    return pl.pallas_call(
        _scan_sample_kernel,
        grid=(steps,),
        in_specs=[pl.BlockSpec((NH, nrow, nrow), lambda s: (s, 0, 0)),
                  pl.BlockSpec((NH, nrow + DH, nrow), lambda s: (s, 0, 0)),
                  tok, tok, tok, pl.BlockSpec((nrow, 128), lambda s: (s, 0)), st],
        out_specs=[tok, st],
        out_shape=[jax.ShapeDtypeStruct((BS * SPAD, QKW), F32),
                   jax.ShapeDtypeStruct((BS, NH, DH, DH), F32)],
        compiler_params=_params("arbitrary"),
        name="delta_scan_sample",
    )(t, l2, vb, kbd, qd, cd, s0)


TO = 256
NPO = RP // TO
TPO = TP // TO
POOL_PREV = 16


def _resident(shape):
    return pl.BlockSpec(shape, lambda i: (0,) * len(shape), pipeline_mode=pl.Buffered(1))


def _branch_a_act(o_ref, z_ref, og):
    parts = []
    for h in range(NH):
        cols = slice(h * DH, (h + 1) * DH)
        parts.append((_rms(o_ref[:, cols], og) * _silu(z_ref[:, cols])).astype(BF16))
    return jnp.concatenate(parts, axis=1)


def _branch_b_proj(yp, pw_ref, ps_ref, wb_ref):
    parts = [_dot(y.astype(BF16), pw_ref[gi]) for gi, y in enumerate(yp)]
    y = jnp.concatenate(parts, axis=1) * ps_ref[...]
    return _dot(y.astype(BF16), wb_ref[...])


def _branch_merge_prompt_kernel(o_ref, z_ref, x_ref, prev_ref, ga_ref, gb_ref,
                                og_ref, wa_ref, pw_ref, ps_ref, wb_ref, m_ref):
    k = pl.program_id(0) % TPO
    ya = _dot(_branch_a_act(o_ref, z_ref, og_ref[...]), wa_ref[...])

    x = x_ref[...]
    v = jnp.concatenate([jnp.where(k > 0, prev_ref[...], 0.0), x], axis=0)
    pos = k * TO + lax.broadcasted_iota(jnp.int32, (TO, 1), 0)
    yp = []
    for gi, win in enumerate(POOL_WINDOWS):
        cols = slice(gi * PGRP, (gi + 1) * PGRP)
        s = v[:, cols]
        shift = 1
        while shift < win:
            s = s + pltpu.roll(s, shift, axis=0)
            shift *= 2
        cnt = jnp.minimum(win, pos + 1).astype(F32)
        yp.append(s[POOL_PREV:] / cnt - x[:, cols])
    yb = _branch_b_proj(yp, pw_ref, ps_ref, wb_ref)

    m_ref[...] = (_sigmoid(ga_ref[...]) * ya + _sigmoid(gb_ref[...]) * yb).astype(BF16)


def _branch_merge_sample_kernel(o_ref, z_ref, x_ref, hist_ref, ga_ref, gb_ref,
                                og_ref, wa_ref, pw_ref, ps_ref, wb_ref, buf_ref, m_ref):
    del buf_ref
    ya = _dot(_branch_a_act(o_ref, z_ref, og_ref[...]), wa_ref[...])

    def slab(src, cols):
        if src >= PHIST:
            return x_ref[(src - PHIST) * BS:(src - PHIST + 1) * BS, cols]
        return hist_ref[src * BS:(src + 1) * BS, cols]

    yp = []
    for gi, win in enumerate(POOL_WINDOWS):
        cols = slice(gi * PGRP, (gi + 1) * PGRP)
        slabs = []
        for t in range(TS):
            acc = slab(PHIST + t, cols)
            for d in range(1, win):
                acc = acc + slab(PHIST + t - d, cols)
            slabs.append(acc / float(win) - slab(PHIST + t, cols))
        yp.append(jnp.concatenate(slabs, axis=0))
    yb = _branch_b_proj(yp, pw_ref, ps_ref, wb_ref)

    m_ref[...] = (_sigmoid(ga_ref[...]) * ya + _sigmoid(gb_ref[...]) * yb).astype(BF16)


def _branch_merge(o_p, o_s, proj1, proj3, hist_s, o_norm_g, w_proj_a, pool_w, pool_scale, w_proj_b):
    assert all(w & (w - 1) == 0 and w <= POOL_PREV for w in POOL_WINDOWS)
    xcol = 2 * D // PW
    weights = [_resident((1, DH)), _resident((QKW, D)), _resident((len(POOL_WINDOWS), PGRP, PGRP)),
               _resident((1, PW)), _resident((PW, D))]
    wargs = (o_norm_g, w_proj_a, pool_w, pool_scale, w_proj_b)
    merged = pl.pallas_call(
        _branch_merge_prompt_kernel,
        grid=(NPO,),
        in_specs=[pl.BlockSpec((TO, QKW), lambda i: (i, 0)),
                  pl.BlockSpec((TO, QKW), lambda i: (i, 3)),
                  pl.BlockSpec((TO, PW), lambda i: (i, xcol)),
                  pl.BlockSpec((POOL_PREV, PW), lambda i: (jnp.maximum(i * (TO // POOL_PREV) - 1, 0), xcol)),
                  pl.BlockSpec((TO, D), lambda i: (i, 0)),
                  pl.BlockSpec((TO, D), lambda i: (i, 1))] + weights,
        out_specs=pl.BlockSpec((TO, D), lambda i: (i, 0)),
        out_shape=jax.ShapeDtypeStruct((R, D), BF16),
        compiler_params=_params("arbitrary"),
        name="branch_merge_prompt",
    )(o_p, proj1, proj3, proj3, proj3, proj3, *wargs)
    sblk = RP // RS
    return pl.pallas_call(
        _branch_merge_sample_kernel,
        grid=(1,),
        in_specs=[pl.BlockSpec((RS, QKW), lambda i: (0, 0)),
                  pl.BlockSpec((RS, QKW), lambda i: (sblk, 3)),
                  pl.BlockSpec((RS, PW), lambda i: (sblk, xcol)),
                  pl.BlockSpec((PHIST * BS, PW), lambda i: (0, 0)),
                  pl.BlockSpec((RS, D), lambda i: (sblk, 0)),
                  pl.BlockSpec((RS, D), lambda i: (sblk, 1))] + weights + [pl.BlockSpec(memory_space=pl.ANY)],
        out_specs=pl.BlockSpec((RS, D), lambda i: (sblk, 0)),
        out_shape=jax.ShapeDtypeStruct((R, D), BF16),
        input_output_aliases={11: 0},
        compiler_params=_params("arbitrary"),
        name="branch_merge_sample",
    )(o_s, proj1, proj3, hist_s, proj3, proj3, *wargs, merged)


NCH = 512


def _proj_residual(a_ref, w_ref, x_ref, gate, out_ref):
    ss = 0.0
    for n in range(D // NCH):
        cols = slice(n * NCH, (n + 1) * NCH)
        r = x_ref[:, cols] + gate[:, cols] * _dot(a_ref[...], w_ref[:, cols])
        out_ref[:, cols] = r
        ss = ss + jnp.sum(r * r, axis=-1, keepdims=True)
    return ss


def _tile_rows(v, rows):
    return jnp.concatenate([v] * (rows // BS), axis=0)


def _post_out_kernel(m_ref, xp_ref, xs_ref,
                     gtp_ref, shp_ref, scp_ref, gts_ref, shs_ref, scs_ref, g2_ref, w_ref,
                     x1_ref, h_ref):
    i = pl.program_id(0)

    def body(x_ref, gt, sh, sc):
        ss = _proj_residual(m_ref, w_ref, x_ref, gt, x1_ref)
        inv = lax.rsqrt(ss * (1.0 / D) + EPS)
        h_ref[...] = (x1_ref[...] * inv * g2_ref[...] * (1.0 + sc) + sh).astype(BF16)

    @pl.when(i < NPT)
    def _():
        body(xp_ref, gtp_ref[0], shp_ref[0], scp_ref[0])

    @pl.when(i == NPT)
    def _():
        body(xs_ref, _tile_rows(gts_ref[...], TM), _tile_rows(shs_ref[...], TM), _tile_rows(scs_ref[...], TM))


def _post_out(merged, xp2, xs2, mod_p, mod_s, norm2_g, w_out):
    gtp, gts = _mod_specs(2, TPB)
    shp, shs = _mod_specs(3, TPB)
    scp, scs = _mod_specs(4, TPB)
    tile = pl.BlockSpec((TM, D), lambda i: (i, 0))
    return pl.pallas_call(
        _post_out_kernel,
        grid=(NPT + 1,),
        in_specs=[tile,
                  pl.BlockSpec((TM, D), lambda i: (jnp.minimum(i, NPT - 1), 0)),
                  _resident((TM, D)),
                  gtp, shp, scp, gts, shs, scs,
                  _resident((1, D)),
                  _resident((D, D))],
        out_specs=[tile, tile],
        out_shape=[jax.ShapeDtypeStruct((R, D), F32), jax.ShapeDtypeStruct((R, D), BF16)],
        compiler_params=_params("arbitrary"),
        name="out_proj_norm2",
    )(merged, xp2, xs2, mod_p, mod_p, mod_p, mod_s, mod_s, mod_s, norm2_g, w_out)


FF_TN = 512


def _ffn1_kernel(h_ref, wg_ref, wu_ref, a_ref, wgb_ref, wub_ref):
    @pl.when(pl.program_id(1) == 0)
    def _():
        wgb_ref[...] = wg_ref[...].astype(BF16)
        wub_ref[...] = wu_ref[...].astype(BF16)

    h = h_ref[...]
    a_ref[...] = (_silu(_dot(h, wgb_ref[...])) * _dot(h, wub_ref[...])).astype(BF16)


def _ffn1(h, w_gate_up):
    nj = DFF // FF_TN
    return pl.pallas_call(
        _ffn1_kernel,
        grid=(nj, R // TMM),
        in_specs=[pl.BlockSpec((TMM, D), lambda j, i: (i, 0)),
                  pl.BlockSpec((D, FF_TN), lambda j, i: (0, j)),
                  pl.BlockSpec((D, FF_TN), lambda j, i: (0, nj + j))],
        out_specs=pl.BlockSpec((TMM, FF_TN), lambda j, i: (i, j)),
        out_shape=jax.ShapeDtypeStruct((R, DFF), BF16),
        scratch_shapes=[pltpu.VMEM((D, FF_TN), BF16), pltpu.VMEM((D, FF_TN), BF16)],
        compiler_params=_params("arbitrary", "arbitrary"),
        name="ffn_gate_up",
    )(h, w_gate_up, w_gate_up)


def _ffn2_kernel(a_ref, w_ref, x1_ref, gtp_ref, gts_ref, fg_ref, yp_ref, ys_ref):
    i = pl.program_id(0)

    def body(gt, y_ref):
        ss = _proj_residual(a_ref, w_ref, x1_ref, gt, y_ref)
        y_ref[...] = y_ref[...] * lax.rsqrt(ss * (1.0 / D) + EPS) * fg_ref[...]

    @pl.when(i < NPO)
    def _():
        body(gtp_ref[0], yp_ref)

    @pl.when(i >= NPO)
    def _():
        body(_tile_rows(gts_ref[...], TO), ys_ref)


def _ffn2(act, w_down, x1, mod_p, mod_s, final_g):
    gtp, gts = _mod_specs(5, TPO)
    return pl.pallas_call(
        _ffn2_kernel,
        grid=(R // TO,),
        in_specs=[pl.BlockSpec((TO, DFF), lambda i: (i, 0)),
                  _resident((DFF, D)),
                  pl.BlockSpec((TO, D), lambda i: (i, 0)),
                  gtp, gts,
                  _resident((1, D))],
        out_specs=[pl.BlockSpec((TO, D), lambda i: (jnp.minimum(i, NPO - 1), 0)),
                   pl.BlockSpec((TO, D), lambda i: (jnp.maximum(i - NPO, 0), 0))],
        out_shape=[jax.ShapeDtypeStruct((RP, D), F32), jax.ShapeDtypeStruct((RS, D), F32)],
        compiler_params=_params("arbitrary"),
        name="ffn_down_final",
    )(act, w_down, x1, mod_p, mod_s, final_g)


def _to_time_major(x):
    return jnp.transpose(x, (1, 0, 2)).reshape(TS * BS, x.shape[-1])


def _to_batch_major(x):
    return jnp.transpose(x.reshape(TS, BS, x.shape[-1]), (1, 0, 2))


def _pad_lanes(v):
    return jnp.zeros((1, 128), F32).at[0, :NH].set(v.astype(F32))


def kernel(x_prompt, x_sample, c_prompt, c_sample, state_delta, state_conv, state_pool, w_ada, b_ada, norm1_g,
           w_in, conv_w, a_log, dt_bias, o_norm_g, pool_w, pool_scale, w_proj_a, w_proj_b, w_out, norm2_g,
           w_gate_up, w_down, final_g):
    assert w_ada.shape[0] == 1, "single layer"
    xp2 = x_prompt.reshape(RP, D)
    xs2 = _to_time_major(x_sample)

    c_all = jnp.concatenate([c_sample, c_prompt, jnp.zeros((4, D), F32)], axis=0)
    mod = _mod(c_all, w_ada[0], b_ada[0].reshape(1, 6 * D))
    mod_s = mod[:BS]
    mod_p = mod[BS:BS + BP].reshape(BP, 1, 6 * D)

    u = _pre(xp2, xs2, mod_p, mod_s, norm1_g[0].reshape(1, D))

    w_in0 = w_in[0]
    o_ab = 4 * QKW
    w_ab = jnp.zeros((D, 256), F32)
    w_ab = w_ab.at[:, :NH].set(w_in0[:, o_ab:o_ab + NH]).at[:, 128:128 + NH].set(w_in0[:, o_ab + NH:o_ab + 2 * NH])
    proj1 = _matmul(u, w_in0, 4 * QKW, 1024, "in_proj_qkvz")
    proj_ab = _matmul(u, w_ab, 256, 256, "in_proj_ab")
    proj3 = _matmul_shifted(u, w_in0, o_ab, 2 * NH, (PW + 2 * D) // 1024, PW // 1024, 1024, "in_proj_gates_pool")

    cw = conv_w[0]
    alog = _pad_lanes(a_log[0])
    dtb = _pad_lanes(dt_bias[0])

    m_p, l2_p, vb_p, kbd_p, qd_p, cd_p = _prep_prompt(proj1, proj_ab, cw, alog, dtb)
    t_p = _solve(jnp.transpose(m_p, (1, 2, 0)), "tri_solve_prompt")
    t_p = jnp.transpose(t_p, (2, 0, 1)).astype(BF16)
    o_p, s_p = _scan_prompt(t_p, l2_p, vb_p, kbd_p, qd_p, cd_p)

    qkv_s = _to_batch_major(proj1[RP:, :CONVC])
    conv_in = jnp.concatenate([state_conv[0], qkv_s, jnp.zeros((BS, SPAD - TS - (CONVW - 1), CONVC), F32)], axis=1)
    ab_s = _to_batch_major(proj_ab[RP:])
    ab_s = jnp.concatenate([ab_s, jnp.zeros((BS, SPAD - TS, 256), F32)], axis=1)
    m_s, l2_s, vb_s, kbd_s, qd_s, cd_s = _prep_sample(
        conv_in.reshape(BS * SPAD, CONVC), ab_s.reshape(BS * SPAD, 256), cw, alog, dtb)
    nblk = BS // SSEQ
    m_blocks = jnp.diagonal(m_s.reshape(nblk * NH, SSEQ, SPAD, SSEQ, SPAD), axis1=1, axis2=3)
    m_blocks = jnp.transpose(m_blocks, (1, 2, 0, 3)).reshape(SPAD, SPAD, nblk * NH * SSEQ)
    t_blocks = _solve(m_blocks, "tri_solve_sample").reshape(SPAD, SPAD, nblk * NH, SSEQ)
    t_s = jnp.einsum("ijgb,bc->gbicj", t_blocks, jnp.eye(SSEQ, dtype=F32))
    t_s = t_s.reshape(nblk * NH, SSEQ * SPAD, SSEQ * SPAD).astype(BF16)
    o_s8, s_s = _scan_sample(t_s, l2_s, vb_s, kbd_s, qd_s, cd_s, state_delta[0])
    o_s = _to_time_major(o_s8.reshape(BS, SPAD, QKW)[:, :TS])

    hist_s = jnp.transpose(state_pool[0], (1, 0, 2)).reshape(PHIST * BS, PW)
    merged = _branch_merge(o_p, o_s, proj1, proj3, hist_s, o_norm_g[0].reshape(1, DH), w_proj_a[0].astype(BF16),
                           pool_w[0].astype(BF16), pool_scale[0].reshape(1, PW), w_proj_b[0].astype(BF16))
    x1, h = _post_out(merged, xp2, xs2, mod_p, mod_s, norm2_g[0].reshape(1, D), w_out[0].astype(BF16))
    act = _ffn1(h, w_gate_up[0])
    y_p, y_s = _ffn2(act, w_down[0].astype(BF16), x1, mod_p, mod_s, final_g.reshape(1, D))

    xpool_s = _to_batch_major(proj3[RP:, 2 * D:])

    def tail_rows(a, n, c0, c1):
        return jnp.stack([a[(b + 1) * TP - n:(b + 1) * TP, c0:c1] for b in range(BP)])

    return (y_p.reshape(BP, TP, D),
            _to_batch_major(y_s),
            s_p[None],
            tail_rows(proj1, CONVW - 1, 0, CONVC)[None],
            tail_rows(proj3, PHIST, 2 * D, 2 * D + PW)[None],
            s_s[None],
            conv_in[:, TS:TS + CONVW - 1][None],
            jnp.concatenate([state_pool[0][:, TS:], xpool_s], axis=1)[None])
```

```python
import functools

import jax
import jax.numpy as jnp
from jax import lax
from jax.experimental import pallas as pl
from jax.experimental.pallas import tpu as pltpu

F32 = jnp.float32
BF16 = jnp.bfloat16

D = 2048
NH = 8
DH = 128
QKW = NH * DH
CONVC = 3 * QKW
CONVW = 4
PW = 1024
PGRP = 256
POOL_WINDOWS = (2, 4, 8, 16)
PHIST = 15
DFF = 5632
EPS = 1e-6

BP, TP = 4, 2048
BS, TS = 128, 4
RP = BP * TP
RS = BS * TS
R = RP + RS
TM = 512
NPT = RP // TM
TPB = TP // TM
TMM = R // 8
CHUNK = 64
SPAD = 8
SSEQ = 16

VMEM_LIMIT = 56 * 1024 * 1024

NT_DIMS = (((1,), (1,)), ((), ()))


def _params(*sem):
    return pltpu.CompilerParams(dimension_semantics=sem, vmem_limit_bytes=VMEM_LIMIT)


def _sigmoid(x):
    return 1.0 / (1.0 + jnp.exp(-x))


def _silu(x):
    return x * _sigmoid(x)


def _softplus(x):
    return jnp.maximum(x, 0.0) + jnp.log1p(jnp.exp(-jnp.abs(x)))


def _rms(x, gain):
    return x * lax.rsqrt(jnp.mean(x * x, axis=-1, keepdims=True) + EPS) * gain


def _dot(a, b):
    return jnp.dot(a, b, preferred_element_type=F32)


def _mod_kernel(c_ref, w_ref, b_ref, o_ref):
    c = c_ref[...]
    o_ref[...] = _dot(_silu(c).astype(BF16), w_ref[...].astype(BF16)) + b_ref[...]


def _mod(c_all, w_ada, b_ada):
    n = c_all.shape[0]
    tn = 1024
    return pl.pallas_call(
        _mod_kernel,
        grid=(6 * D // tn,),
        in_specs=[pl.BlockSpec((n, D), lambda j: (0, 0)),
                  pl.BlockSpec((D, tn), lambda j: (0, j)),
                  pl.BlockSpec((1, tn), lambda j: (0, j))],
        out_specs=pl.BlockSpec((n, tn), lambda j: (0, j)),
        out_shape=jax.ShapeDtypeStruct((n, 6 * D), F32),
        compiler_params=_params("arbitrary"),
        name="ada_mod",
    )(c_all, w_ada, b_ada)


def _mod_specs(col, tiles_per_seq):
    p = pl.BlockSpec((1, 1, D), lambda i, *_: (jnp.minimum(i // tiles_per_seq, BP - 1), 0, col))
    s = pl.BlockSpec((BS, D), lambda i, *_: (0, col))
    return p, s


def _pre_kernel(xp_ref, xs_ref, shp_ref, scp_ref, shs_ref, scs_ref, g_ref, u_ref):
    i = pl.program_id(0)
    g = g_ref[...]

    @pl.when(i < NPT)
    def _():
        u = _rms(xp_ref[...], g) * (1.0 + scp_ref[0]) + shp_ref[0]
        u_ref[...] = u.astype(BF16)

    @pl.when(i == NPT)
    def _():
        for t in range(TS):
            rows = slice(t * BS, (t + 1) * BS)
            u = _rms(xs_ref[rows, :], g) * (1.0 + scs_ref[...]) + shs_ref[...]
            u_ref[rows, :] = u.astype(BF16)


def _pre(xp2, xs2, mod_p, mod_s, g):
    shp, shs = _mod_specs(0, TPB)
    scp, scs = _mod_specs(1, TPB)
    return pl.pallas_call(
        _pre_kernel,
        grid=(NPT + 1,),
        in_specs=[pl.BlockSpec((TM, D), lambda i: (jnp.minimum(i, NPT - 1), 0)),
                  pl.BlockSpec((TM, D), lambda i: (0, 0)),
                  shp, scp, shs, scs,
                  pl.BlockSpec((1, D), lambda i: (0, 0))],
        out_specs=pl.BlockSpec((TM, D), lambda i: (i, 0)),
        out_shape=jax.ShapeDtypeStruct((R, D), BF16),
        compiler_params=_params("arbitrary"),
        name="norm1_mod",
    )(xp2, xs2, mod_p, mod_p, mod_s, mod_s, g)


def _mm_kernel(a_ref, w_ref, o_ref, wb_ref):
    @pl.when(pl.program_id(1) == 0)
    def _():
        wb_ref[...] = w_ref[...].astype(BF16)

    o_ref[...] = _dot(a_ref[...], wb_ref[...])


def _matmul(a, w, n_cols, tn, name):
    rows, k = a.shape
    return pl.pallas_call(
        _mm_kernel,
        grid=(n_cols // tn, rows // TMM),
        in_specs=[pl.BlockSpec((TMM, k), lambda j, i: (i, 0)),
                  pl.BlockSpec((k, tn), lambda j, i: (0, j))],
        out_specs=pl.BlockSpec((TMM, tn), lambda j, i: (i, j)),
        out_shape=jax.ShapeDtypeStruct((rows, n_cols), F32),
        scratch_shapes=[pltpu.VMEM((k, tn), BF16)],
        compiler_params=_params("arbitrary", "arbitrary"),
        name=name,
    )(a, w)


def _mm_shift_kernel(shift, a_ref, w0_ref, w1_ref, o_ref, wb_ref):
    @pl.when(pl.program_id(1) == 0)
    def _():
        w = jnp.concatenate([w0_ref[:, shift:], w1_ref[:, :shift]], axis=1)
        wb_ref[...] = w.astype(BF16)

    o_ref[...] = _dot(a_ref[...], wb_ref[...])


def _matmul_shifted(a, w, col0, shift, nt, rot, tn, name):
    rows, k = a.shape
    src = lambda j: lax.rem(j + rot, nt)
    return pl.pallas_call(
        functools.partial(_mm_shift_kernel, shift),
        grid=(nt, rows // TMM),
        in_specs=[pl.BlockSpec((TMM, k), lambda j, i: (i, 0)),
                  pl.BlockSpec((k, tn), lambda j, i: (0, col0 // tn + src(j))),
                  pl.BlockSpec((k, 128), lambda j, i: (0, (col0 + (src(j) + 1) * tn) // 128))],
        out_specs=pl.BlockSpec((TMM, tn), lambda j, i: (i, j)),
        out_shape=jax.ShapeDtypeStruct((rows, nt * tn), F32),
        scratch_shapes=[pltpu.VMEM((k, tn), BF16)],
        compiler_params=_params("arbitrary", "arbitrary"),
        name=name,
    )(a, w, w)


def _delta_prep(get_qkv, ab, alog, dtb, valid, chunk, seq_block, put_m, l2_ref, vb_ref, kbd_ref, qd_ref, cd_ref):
    rows = ab.shape[0]
    a = ab[:, :128]
    b = ab[:, 128:]
    beta = _sigmoid(b)
    g = -jnp.exp(alog) * _softplus(a + dtb)
    if valid is not None:
        beta = jnp.where(valid, beta, 0.0)
        g = jnp.where(valid, g, 0.0)
    shift = seq_block.bit_length() - 1
    r = lax.broadcasted_iota(jnp.int32, (rows, rows), 0)
    c = lax.broadcasted_iota(jnp.int32, (rows, rows), 1)
    same = (r >> shift) == (c >> shift)
    ltri = jnp.where(same & (r >= c), 1.0, 0.0)
    lall = jnp.where(same, 1.0, 0.0)
    dcol = jnp.dot(ltri, g, preferred_element_type=F32, precision=lax.Precision.HIGHEST)
    last = jnp.dot(lall, g, preferred_element_type=F32, precision=lax.Precision.HIGHEST)
    cd_ref[...] = jnp.exp(last)
    drow = dcol.T
    edc = jnp.exp(dcol)
    etl = jnp.exp(last - dcol)
    rc = lax.broadcasted_iota(jnp.int32, (chunk, chunk), 0)
    cc = lax.broadcasted_iota(jnp.int32, (chunk, chunk), 1)
    same_c = (rc >> shift) == (cc >> shift)
    causal = same_c & (rc >= cc)
    strict = same_c & (rc > cc)
    for h in range(NH):
        cols = slice(h * DH, (h + 1) * DH)
        for sc in range(rows // chunk):
            rs = slice(sc * chunk, (sc + 1) * chunk)
            q, k, v = get_qkv(h, sc)
            qn = q * lax.rsqrt(jnp.sum(q * q, axis=-1, keepdims=True) + EPS) * (DH ** -0.5)
            kn = k * lax.rsqrt(jnp.sum(k * k, axis=-1, keepdims=True) + EPS)
            bcol = beta[rs, h:h + 1]
            ed = edc[rs, h:h + 1]
            kb = kn * bcol
            vb_ref[rs, cols] = (v * bcol).astype(BF16)
            kbd_ref[rs, cols] = (kb * ed).astype(BF16)
            qd_ref[rs, cols] = (qn * ed).astype(BF16)
            kt = kn * etl[rs, h:h + 1]
            diff = dcol[rs, h:h + 1] - drow[h:h + 1, rs]
            lm = jnp.where(causal, jnp.exp(jnp.where(causal, diff, 0.0)), 0.0)
            lhs = jnp.concatenate([kb, qn], axis=0).astype(BF16)
            s = lax.dot_general(lhs, kn.astype(BF16), NT_DIMS, preferred_element_type=F32)
            idx = sc * NH + h
            put_m(idx, jnp.where(strict, s[:chunk] * lm, 0.0))
            l2_ref[idx] = jnp.concatenate([s[chunk:] * lm, kt.T], axis=0).astype(BF16)


LANES = 128
CPB = TM // CHUNK * NH
assert 2 * CPB == LANES and TPB % 2 == 0


def _solve_lanes(mt_ref, xt_ref):
    xt_ref[...] = jnp.zeros_like(xt_ref)
    half = CHUNK // 2

    def rows(i0, i1, c0, c1, j0):
        rowid = lax.broadcasted_iota(jnp.int32, (c1 - c0, LANES), 0) + c0

        def body_i(i, carry):
            n = i - j0

            def pair(p, acc):
                j = j0 + 2 * p
                acc = acc - mt_ref[i, pl.ds(j, 1), :] * xt_ref[j, c0:c1, :]
                return acc - mt_ref[i, pl.ds(j + 1, 1), :] * xt_ref[j + 1, c0:c1, :]

            acc = lax.fori_loop(0, n // 2, pair, (rowid == i).astype(F32))
            jl = jnp.maximum(i - 1, j0)
            m_last = jnp.where((n & 1) == 1, mt_ref[i, pl.ds(jl, 1), :], 0.0)
            xt_ref[i, c0:c1, :] = acc - m_last * xt_ref[jl, c0:c1, :]
            return carry

        lax.fori_loop(i0, i1, body_i, 0)

    rows(0, half, 0, half, 0)
    rows(half, CHUNK, 0, half, 0)
    rows(half, CHUNK, half, CHUNK, half)


def _prep_prompt_kernel(cur_ref, prev_ref, ab_ref, cw_ref, alog_ref, dtb_ref,
                        t_ref, l2_ref, vb_ref, kbd_ref, qd_ref, cd_ref, xh_ref, mall_ref, mt_ref, xt_ref):
    blk = pl.program_id(1)
    odd = blk % 2
    xh_ref[0:8, :] = jnp.where(blk > 0, prev_ref[...], 0.0)
    xh_ref[8:8 + CHUNK, :] = cur_ref[0:CHUNK, :]
    w = cw_ref[...]

    def conv(sc, cols):
        src, base = (xh_ref, 8) if sc == 0 else (cur_ref, sc * CHUNK)
        y = src[base:base + CHUNK, cols] * w[CONVW - 1:CONVW, cols]
        for j in range(CONVW - 1):
            lo = base - (CONVW - 1) + j
            y = y + src[lo:lo + CHUNK, cols] * w[j:j + 1, cols]
        return _silu(y)

    def get_qkv(h, sc):
        return tuple(conv(sc, slice(p * QKW + h * DH, p * QKW + (h + 1) * DH)) for p in range(3))

    def put_m(idx, m):
        start = pl.multiple_of((odd * CPB + idx) * CHUNK, CHUNK)
        mall_ref[pl.ds(start, CHUNK), :] = m

    _delta_prep(get_qkv, ab_ref[...], alog_ref[...], dtb_ref[...], None, CHUNK, CHUNK,
                put_m, l2_ref, vb_ref, kbd_ref, qd_ref, cd_ref)

    @pl.when(odd == 1)
    def _():
        for i in range(CHUNK):
            mt_ref[i] = mall_ref[pl.ds(i, LANES, stride=CHUNK), :].T
        _solve_lanes(mt_ref, xt_ref)
        for i in range(CHUNK):
            mall_ref[pl.ds(i, LANES, stride=CHUNK), :] = xt_ref[i].T
        t_ref[...] = mall_ref[...].astype(BF16)


def _prep_prompt(proj1, proj_ab, conv_w, alog, dtb):
    nch = RP // CHUNK * NH
    row = lambda b, k: b * TPB + k
    tok = pl.BlockSpec((TM, QKW), lambda b, k: (row(b, k), 0))
    t, *rest = pl.pallas_call(
        _prep_prompt_kernel,
        grid=(BP, TPB),
        in_specs=[pl.BlockSpec((TM, CONVC), lambda b, k: (row(b, k), 0)),
                  pl.BlockSpec((8, CONVC), lambda b, k: (jnp.maximum(row(b, k) * (TM // 8) - 1, 0), 0)),
                  pl.BlockSpec((TM, 256), lambda b, k: (row(b, k), 0)),
                  pl.BlockSpec((CONVW, CONVC), lambda b, k: (0, 0)),
                  pl.BlockSpec((1, 128), lambda b, k: (0, 0)),
                  pl.BlockSpec((1, 128), lambda b, k: (0, 0))],
        out_specs=[pl.BlockSpec((LANES * CHUNK, CHUNK), lambda b, k: (row(b, k) // 2, 0)),
                   pl.BlockSpec((CPB, CHUNK + DH, CHUNK), lambda b, k: (row(b, k), 0, 0)),
                   tok, tok, tok,
                   pl.BlockSpec((TM, 128), lambda b, k: (row(b, k), 0))],
        out_shape=[jax.ShapeDtypeStruct((nch * CHUNK, CHUNK), BF16),
                   jax.ShapeDtypeStruct((nch, CHUNK + DH, CHUNK), BF16),
                   jax.ShapeDtypeStruct((RP, QKW), BF16),
                   jax.ShapeDtypeStruct((RP, QKW), BF16),
                   jax.ShapeDtypeStruct((RP, QKW), BF16),
                   jax.ShapeDtypeStruct((RP, 128), F32)],
        scratch_shapes=[pltpu.VMEM((8 + CHUNK, CONVC), F32),
                        pltpu.VMEM((LANES * CHUNK, CHUNK), F32),
                        pltpu.VMEM((CHUNK, CHUNK, LANES), F32),
                        pltpu.VMEM((CHUNK, CHUNK, LANES), F32)],
        compiler_params=_params("arbitrary", "arbitrary"),
        name="delta_prep_prompt",
    )(proj1, proj1, proj_ab, conv_w, alog, dtb)
    return (t.reshape(nch, CHUNK, CHUNK), *rest)


def _prep_sample_kernel(xp_ref, ab_ref, cw_ref, alog_ref, dtb_ref,
                        m_ref, l2_ref, vb_ref, kbd_ref, qd_ref, cd_ref):
    nrow = SSEQ * SPAD
    w = cw_ref[...]
    valid = (lax.broadcasted_iota(jnp.int32, (nrow, 1), 0) & (SPAD - 1)) < TS

    def conv(cols):
        x = xp_ref[:, cols]
        y = x * w[0:1, cols]
        for j in range(1, CONVW):
            y = y + pltpu.roll(x, nrow - j, axis=0) * w[j:j + 1, cols]
        return jnp.where(valid, _silu(y), 0.0)

    def get_qkv(h, sc):
        del sc
        return tuple(conv(slice(p * QKW + h * DH, p * QKW + (h + 1) * DH)) for p in range(3))

    def put_m(idx, m):
        m_ref[idx] = m

    _delta_prep(get_qkv, ab_ref[...], alog_ref[...], dtb_ref[...], valid, nrow, SPAD,
                put_m, l2_ref, vb_ref, kbd_ref, qd_ref, cd_ref)


def _prep_sample(xp8, ab8, conv_w, alog, dtb):
    nrow = SSEQ * SPAD
    steps = BS // SSEQ
    tok = pl.BlockSpec((nrow, QKW), lambda s: (s, 0))
    mat = pl.BlockSpec((NH, nrow, nrow), lambda s: (s, 0, 0))
    return pl.pallas_call(
        _prep_sample_kernel,
        grid=(steps,),
        in_specs=[pl.BlockSpec((nrow, CONVC), lambda s: (s, 0)),
                  pl.BlockSpec((nrow, 256), lambda s: (s, 0)),
                  pl.BlockSpec((CONVW, CONVC), lambda s: (0, 0)),
                  pl.BlockSpec((1, 128), lambda s: (0, 0)),
                  pl.BlockSpec((1, 128), lambda s: (0, 0))],
        out_specs=[mat, pl.BlockSpec((NH, nrow + DH, nrow), lambda s: (s, 0, 0)),
                   tok, tok, tok, pl.BlockSpec((nrow, 128), lambda s: (s, 0))],
        out_shape=[jax.ShapeDtypeStruct((steps * NH, nrow, nrow), F32),
                   jax.ShapeDtypeStruct((steps * NH, nrow + DH, nrow), BF16),
                   jax.ShapeDtypeStruct((BS * SPAD, QKW), BF16),
                   jax.ShapeDtypeStruct((BS * SPAD, QKW), BF16),
                   jax.ShapeDtypeStruct((BS * SPAD, QKW), BF16),
                   jax.ShapeDtypeStruct((BS * SPAD, 128), F32)],
        compiler_params=_params("arbitrary"),
        name="delta_prep_sample",
    )(xp8, ab8, conv_w, alog, dtb)


def _solve_kernel(m_ref, x_ref):
    n, _, lanes = m_ref.shape
    rowid = lax.broadcasted_iota(jnp.int32, (n, lanes), 0)

    def body_i(i, carry):
        def body_j(j, acc):
            return acc - m_ref[i, pl.ds(j, 1), :] * x_ref[j]

        x_ref[i] = lax.fori_loop(0, i, body_j, (rowid == i).astype(F32))
        return carry

    lax.fori_loop(0, n, body_i, 0)


def _solve(m, name):
    n, _, g = m.shape
    lanes = 128
    spec = pl.BlockSpec((n, n, lanes), lambda s: (0, 0, s))
    return pl.pallas_call(
        _solve_kernel,
        grid=(g // lanes,),
        in_specs=[spec],
        out_specs=spec,
        out_shape=jax.ShapeDtypeStruct((n, n, g), F32),
        compiler_params=_params("arbitrary"),
        name=name,
    )(m)


CPS = 2


def _scan_prompt_kernel(t_ref, l2_ref, vb_ref, kbd_ref, qd_ref, cd_ref, o_ref, sout_ref, s_ref, ub_ref, wd_ref):
    step = pl.program_id(0)

    @pl.when(step == 0)
    def _():
        s_ref[...] = jnp.zeros_like(s_ref)

    chains = [(b, h) for b in range(BP) for h in range(NH)]
    for c in range(CPS):
        rows = slice(c * CHUNK, (c + 1) * CHUNK)
        for n, (b, h) in enumerate(chains):
            cols = slice(h * DH, (h + 1) * DH)
            rhs = jnp.concatenate([vb_ref[b, rows, cols], kbd_ref[b, rows, cols]], axis=1)
            r1 = _dot(t_ref[b, c * NH + h], rhs)
            ub_ref[c * len(chains) + n] = r1[:, :DH]
            wd_ref[c * len(chains) + n] = r1[:, DH:].astype(BF16)
    for c in range(CPS):
        rows = slice(c * CHUNK, (c + 1) * CHUNK)
        r2s = []
        for n, (b, h) in enumerate(chains):
            cols = slice(h * DH, (h + 1) * DH)
            lhs = jnp.concatenate([wd_ref[c * len(chains) + n], qd_ref[b, rows, cols]], axis=0)
            r2s.append(_dot(lhs, s_ref[n].astype(BF16)))
        for n, (b, h) in enumerate(chains):
            cols = slice(h * DH, (h + 1) * DH)
            r2 = r2s[n]
            u = (ub_ref[c * len(chains) + n] - r2[:CHUNK]).astype(BF16)
            r3 = _dot(l2_ref[b, c * NH + h], u)
            o_ref[b, rows, cols] = r2[CHUNK:] + r3[:CHUNK]
            cd = cd_ref[b, c * CHUNK:c * CHUNK + 1, h:h + 1]
            s_ref[n] = s_ref[n] * cd + r3[CHUNK:]

    @pl.when(step == pl.num_programs(0) - 1)
    def _():
        sout_ref[...] = s_ref[...]


def _scan_prompt(t, l2, vb, kbd, qd, cd):
    nchunk = TP // CHUNK
    rows = CPS * CHUNK
    t = t.reshape(BP, nchunk * NH, CHUNK, CHUNK)
    l2 = l2.reshape(BP, nchunk * NH, CHUNK + DH, CHUNK)
    tok3 = lambda a: a.reshape(BP, TP, a.shape[-1])
    tok = pl.BlockSpec((BP, rows, QKW), lambda s: (0, s, 0))
    o, s_fin = pl.pallas_call(
        _scan_prompt_kernel,
        grid=(nchunk // CPS,),
        in_specs=[pl.BlockSpec((BP, CPS * NH, CHUNK, CHUNK), lambda s: (0, s, 0, 0)),
                  pl.BlockSpec((BP, CPS * NH, CHUNK + DH, CHUNK), lambda s: (0, s, 0, 0)),
                  tok, tok, tok,
                  pl.BlockSpec((BP, rows, 128), lambda s: (0, s, 0))],
        out_specs=[tok, pl.BlockSpec((BP * NH, DH, DH), lambda s: (0, 0, 0))],
        out_shape=[jax.ShapeDtypeStruct((BP, TP, QKW), F32),
                   jax.ShapeDtypeStruct((BP * NH, DH, DH), F32)],
        scratch_shapes=[pltpu.VMEM((BP * NH, DH, DH), F32),
                        pltpu.VMEM((CPS * BP * NH, CHUNK, DH), F32),
                        pltpu.VMEM((CPS * BP * NH, CHUNK, DH), BF16)],
        compiler_params=_params("arbitrary"),
        name="delta_scan_prompt",
    )(t, l2, tok3(vb), tok3(kbd), tok3(qd), tok3(cd))
    return o.reshape(RP, QKW), s_fin.reshape(BP, NH, DH, DH)


def _scan_sample_kernel(t_ref, l2_ref, vb_ref, kbd_ref, qd_ref, cd_ref, s0_ref, o_ref, sout_ref):
    nrow = SSEQ * SPAD
    seq_of_row = lax.broadcasted_iota(jnp.int32, (nrow, 1), 0) >> (SPAD.bit_length() - 1)
    for h in range(NH):
        cols = slice(h * DH, (h + 1) * DH)
        rhs = jnp.concatenate([vb_ref[:, cols], kbd_ref[:, cols]], axis=1)
        r1 = _dot(t_ref[h], rhs)
        qd = qd_ref[:, cols].astype(F32)
        u_parts, qs_parts = [], []
        for b in range(SSEQ):
            rows = slice(b * SPAD, (b + 1) * SPAD)
            lhs = jnp.concatenate([r1[rows, DH:], qd[rows]], axis=0).astype(BF16)
            r2 = _dot(lhs, s0_ref[b, h].astype(BF16))
            u_parts.append(r1[rows, :DH] - r2[:SPAD])
            qs_parts.append(r2[SPAD:])
        u = jnp.concatenate(u_parts, axis=0)
        o_ref[:, cols] = jnp.concatenate(qs_parts, axis=0) + _dot(l2_ref[h, :nrow, :], u.astype(BF16))
        ktt = l2_ref[h, nrow:, :]
        for b in range(SSEQ):
            ub = jnp.where(seq_of_row == b, u, 0.0).astype(BF16)
            cd = cd_ref[b * SPAD:b * SPAD + 1, h:h + 1]
            sout_ref[b, h] = s0_ref[b, h] * cd + _dot(ktt, ub)


def _scan_sample(t, l2, vb, kbd, qd, cd, s0):
    nrow = SSEQ * SPAD
    steps = BS // SSEQ
    tok = pl.BlockSpec((nrow, QKW), lambda s: (s, 0))
    st = pl.BlockSpec((SSEQ, NH, DH, DH), lambda s: (s, 0, 0, 0))
    return pl.pallas_call(
        _scan_sample_kernel,
        grid=(steps,),
        in_specs=[pl.BlockSpec((NH, nrow, nrow), lambda s: (s, 0, 0)),
                  pl.BlockSpec((NH, nrow + DH, nrow), lambda s: (s, 0, 0)),
                  tok, tok, tok, pl.BlockSpec((nrow, 128), lambda s: (s, 0)), st],
        out_specs=[tok, st],
        out_shape=[jax.ShapeDtypeStruct((BS * SPAD, QKW), F32),
                   jax.ShapeDtypeStruct((BS, NH, DH, DH), F32)],
        compiler_params=_params("arbitrary"),
        name="delta_scan_sample",
    )(t, l2, vb, kbd, qd, cd, s0)


TO = 256
NPO = RP // TO
TPO = TP // TO
POOL_PREV = 16


def _resident(shape):
    return pl.BlockSpec(shape, lambda i: (0,) * len(shape), pipeline_mode=pl.Buffered(1))


def _branch_a_act(o_ref, z_ref, og):
    parts = []
    for h in range(NH):
        cols = slice(h * DH, (h + 1) * DH)
        parts.append((_rms(o_ref[:, cols], og) * _silu(z_ref[:, cols])).astype(BF16))
    return jnp.concatenate(parts, axis=1)


def _branch_b_proj(yp, pw_ref, ps_ref, wb_ref):
    parts = [_dot(y.astype(BF16), pw_ref[gi]) for gi, y in enumerate(yp)]
    y = jnp.concatenate(parts, axis=1) * ps_ref[...]
    return _dot(y.astype(BF16), wb_ref[...])


def _branch_merge_prompt_kernel(o_ref, z_ref, x_ref, prev_ref, ga_ref, gb_ref,
                                og_ref, wa_ref, pw_ref, ps_ref, wb_ref, m_ref):
    k = pl.program_id(0) % TPO
    ya = _dot(_branch_a_act(o_ref, z_ref, og_ref[...]), wa_ref[...])

    x = x_ref[...]
    v = jnp.concatenate([jnp.where(k > 0, prev_ref[...], 0.0), x], axis=0)
    pos = k * TO + lax.broadcasted_iota(jnp.int32, (TO, 1), 0)
    yp = []
    for gi, win in enumerate(POOL_WINDOWS):
        cols = slice(gi * PGRP, (gi + 1) * PGRP)
        s = v[:, cols]
        shift = 1
        while shift < win:
            s = s + pltpu.roll(s, shift, axis=0)
            shift *= 2
        cnt = jnp.minimum(win, pos + 1).astype(F32)
        yp.append(s[POOL_PREV:] / cnt - x[:, cols])
    yb = _branch_b_proj(yp, pw_ref, ps_ref, wb_ref)

    m_ref[...] = (_sigmoid(ga_ref[...]) * ya + _sigmoid(gb_ref[...]) * yb).astype(BF16)


def _branch_merge_sample_kernel(o_ref, z_ref, x_ref, hist_ref, ga_ref, gb_ref,
                                og_ref, wa_ref, pw_ref, ps_ref, wb_ref, buf_ref, m_ref):
    del buf_ref
    ya = _dot(_branch_a_act(o_ref, z_ref, og_ref[...]), wa_ref[...])

    def slab(src, cols):
        if src >= PHIST:
            return x_ref[(src - PHIST) * BS:(src - PHIST + 1) * BS, cols]
        return hist_ref[src * BS:(src + 1) * BS, cols]

    yp = []
    for gi, win in enumerate(POOL_WINDOWS):
        cols = slice(gi * PGRP, (gi + 1) * PGRP)
        slabs = []
        for t in range(TS):
            acc = slab(PHIST + t, cols)
            for d in range(1, win):
                acc = acc + slab(PHIST + t - d, cols)
            slabs.append(acc / float(win) - slab(PHIST + t, cols))
        yp.append(jnp.concatenate(slabs, axis=0))
    yb = _branch_b_proj(yp, pw_ref, ps_ref, wb_ref)

    m_ref[...] = (_sigmoid(ga_ref[...]) * ya + _sigmoid(gb_ref[...]) * yb).astype(BF16)


def _branch_merge(o_p, o_s, proj1, proj3, hist_s, o_norm_g, w_proj_a, pool_w, pool_scale, w_proj_b):
    assert all(w & (w - 1) == 0 and w <= POOL_PREV for w in POOL_WINDOWS)
    xcol = 2 * D // PW
    weights = [_resident((1, DH)), _resident((QKW, D)), _resident((len(POOL_WINDOWS), PGRP, PGRP)),
               _resident((1, PW)), _resident((PW, D))]
    wargs = (o_norm_g, w_proj_a, pool_w, pool_scale, w_proj_b)
    merged = pl.pallas_call(
        _branch_merge_prompt_kernel,
        grid=(NPO,),
        in_specs=[pl.BlockSpec((TO, QKW), lambda i: (i, 0)),
                  pl.BlockSpec((TO, QKW), lambda i: (i, 3)),
                  pl.BlockSpec((TO, PW), lambda i: (i, xcol)),
                  pl.BlockSpec((POOL_PREV, PW), lambda i: (jnp.maximum(i * (TO // POOL_PREV) - 1, 0), xcol)),
                  pl.BlockSpec((TO, D), lambda i: (i, 0)),
                  pl.BlockSpec((TO, D), lambda i: (i, 1))] + weights,
        out_specs=pl.BlockSpec((TO, D), lambda i: (i, 0)),
        out_shape=jax.ShapeDtypeStruct((R, D), BF16),
        compiler_params=_params("arbitrary"),
        name="branch_merge_prompt",
    )(o_p, proj1, proj3, proj3, proj3, proj3, *wargs)
    sblk = RP // RS
    return pl.pallas_call(
        _branch_merge_sample_kernel,
        grid=(1,),
        in_specs=[pl.BlockSpec((RS, QKW), lambda i: (0, 0)),
                  pl.BlockSpec((RS, QKW), lambda i: (sblk, 3)),
                  pl.BlockSpec((RS, PW), lambda i: (sblk, xcol)),
                  pl.BlockSpec((PHIST * BS, PW), lambda i: (0, 0)),
                  pl.BlockSpec((RS, D), lambda i: (sblk, 0)),
                  pl.BlockSpec((RS, D), lambda i: (sblk, 1))] + weights + [pl.BlockSpec(memory_space=pl.ANY)],
        out_specs=pl.BlockSpec((RS, D), lambda i: (sblk, 0)),
        out_shape=jax.ShapeDtypeStruct((R, D), BF16),
        input_output_aliases={11: 0},
        compiler_params=_params("arbitrary"),
        name="branch_merge_sample",
    )(o_s, proj1, proj3, hist_s, proj3, proj3, *wargs, merged)


NCH = 512


def _proj_residual(a_ref, w_ref, x_ref, gate, out_ref):
    ss = 0.0
    for n in range(D // NCH):
        cols = slice(n * NCH, (n + 1) * NCH)
        r = x_ref[:, cols] + gate[:, cols] * _dot(a_ref[...], w_ref[:, cols])
        out_ref[:, cols] = r
        ss = ss + jnp.sum(r * r, axis=-1, keepdims=True)
    return ss


def _tile_rows(v, rows):
    return jnp.concatenate([v] * (rows // BS), axis=0)


def _post_out_kernel(m_ref, xp_ref, xs_ref,
                     gtp_ref, shp_ref, scp_ref, gts_ref, shs_ref, scs_ref, g2_ref, w_ref,
                     x1_ref, h_ref):
    i = pl.program_id(0)

    def body(x_ref, gt, sh, sc):
        ss = _proj_residual(m_ref, w_ref, x_ref, gt, x1_ref)
        inv = lax.rsqrt(ss * (1.0 / D) + EPS)
        h_ref[...] = (x1_ref[...] * inv * g2_ref[...] * (1.0 + sc) + sh).astype(BF16)

    @pl.when(i < NPT)
    def _():
        body(xp_ref, gtp_ref[0], shp_ref[0], scp_ref[0])

    @pl.when(i == NPT)
    def _():
        body(xs_ref, _tile_rows(gts_ref[...], TM), _tile_rows(shs_ref[...], TM), _tile_rows(scs_ref[...], TM))


def _post_out(merged, xp2, xs2, mod_p, mod_s, norm2_g, w_out):
    gtp, gts = _mod_specs(2, TPB)
    shp, shs = _mod_specs(3, TPB)
    scp, scs = _mod_specs(4, TPB)
    tile = pl.BlockSpec((TM, D), lambda i: (i, 0))
    return pl.pallas_call(
        _post_out_kernel,
        grid=(NPT + 1,),
        in_specs=[tile,
                  pl.BlockSpec((TM, D), lambda i: (jnp.minimum(i, NPT - 1), 0)),
                  _resident((TM, D)),
                  gtp, shp, scp, gts, shs, scs,
                  _resident((1, D)),
                  _resident((D, D))],
        out_specs=[tile, tile],
        out_shape=[jax.ShapeDtypeStruct((R, D), F32), jax.ShapeDtypeStruct((R, D), BF16)],
        compiler_params=_params("arbitrary"),
        name="out_proj_norm2",
    )(merged, xp2, xs2, mod_p, mod_p, mod_p, mod_s, mod_s, mod_s, norm2_g, w_out)


FF_TN = 512


def _ffn1_kernel(h_ref, wg_ref, wu_ref, a_ref, wgb_ref, wub_ref):
    @pl.when(pl.program_id(1) == 0)
    def _():
        wgb_ref[...] = wg_ref[...].astype(BF16)
        wub_ref[...] = wu_ref[...].astype(BF16)

    h = h_ref[...]
    a_ref[...] = (_silu(_dot(h, wgb_ref[...])) * _dot(h, wub_ref[...])).astype(BF16)


def _ffn1(h, w_gate_up):
    nj = DFF // FF_TN
    return pl.pallas_call(
        _ffn1_kernel,
        grid=(nj, R // TMM),
        in_specs=[pl.BlockSpec((TMM, D), lambda j, i: (i, 0)),
                  pl.BlockSpec((D, FF_TN), lambda j, i: (0, j)),
                  pl.BlockSpec((D, FF_TN), lambda j, i: (0, nj + j))],
        out_specs=pl.BlockSpec((TMM, FF_TN), lambda j, i: (i, j)),
        out_shape=jax.ShapeDtypeStruct((R, DFF), BF16),
        scratch_shapes=[pltpu.VMEM((D, FF_TN), BF16), pltpu.VMEM((D, FF_TN), BF16)],
        compiler_params=_params("arbitrary", "arbitrary"),
        name="ffn_gate_up",
    )(h, w_gate_up, w_gate_up)


def _ffn2_kernel(a_ref, w_ref, x1_ref, gtp_ref, gts_ref, fg_ref, yp_ref, ys_ref):
    i = pl.program_id(0)

    def body(gt, y_ref):
        ss = _proj_residual(a_ref, w_ref, x1_ref, gt, y_ref)
        y_ref[...] = y_ref[...] * lax.rsqrt(ss * (1.0 / D) + EPS) * fg_ref[...]

    @pl.when(i < NPO)
    def _():
        body(gtp_ref[0], yp_ref)

    @pl.when(i >= NPO)
    def _():
        body(_tile_rows(gts_ref[...], TO), ys_ref)


def _ffn2(act, w_down, x1, mod_p, mod_s, final_g):
    gtp, gts = _mod_specs(5, TPO)
    return pl.pallas_call(
        _ffn2_kernel,
        grid=(R // TO,),
        in_specs=[pl.BlockSpec((TO, DFF), lambda i: (i, 0)),
                  _resident((DFF, D)),
                  pl.BlockSpec((TO, D), lambda i: (i, 0)),
                  gtp, gts,
                  _resident((1, D))],
        out_specs=[pl.BlockSpec((TO, D), lambda i: (jnp.minimum(i, NPO - 1), 0)),
                   pl.BlockSpec((TO, D), lambda i: (jnp.maximum(i - NPO, 0), 0))],
        out_shape=[jax.ShapeDtypeStruct((RP, D), F32), jax.ShapeDtypeStruct((RS, D), F32)],
        compiler_params=_params("arbitrary"),
        name="ffn_down_final",
    )(act, w_down, x1, mod_p, mod_s, final_g)


def _to_time_major(x):
    return jnp.transpose(x, (1, 0, 2)).reshape(TS * BS, x.shape[-1])


def _to_batch_major(x):
    return jnp.transpose(x.reshape(TS, BS, x.shape[-1]), (1, 0, 2))


def _pad_lanes(v):
    return jnp.zeros((1, 128), F32).at[0, :NH].set(v.astype(F32))


def kernel(x_prompt, x_sample, c_prompt, c_sample, state_delta, state_conv, state_pool, w_ada, b_ada, norm1_g,
           w_in, conv_w, a_log, dt_bias, o_norm_g, pool_w, pool_scale, w_proj_a, w_proj_b, w_out, norm2_g,
           w_gate_up, w_down, final_g):
    assert w_ada.shape[0] == 1, "single layer"
    xp2 = x_prompt.reshape(RP, D)
    xs2 = _to_time_major(x_sample)

    c_all = jnp.concatenate([c_sample, c_prompt, jnp.zeros((4, D), F32)], axis=0)
    mod = _mod(c_all, w_ada[0], b_ada[0].reshape(1, 6 * D))
    mod_s = mod[:BS]
    mod_p = mod[BS:BS + BP].reshape(BP, 1, 6 * D)

    u = _pre(xp2, xs2, mod_p, mod_s, norm1_g[0].reshape(1, D))

    w_in0 = w_in[0]
    o_ab = 4 * QKW
    w_ab = jnp.zeros((D, 256), F32)
    w_ab = w_ab.at[:, :NH].set(w_in0[:, o_ab:o_ab + NH]).at[:, 128:128 + NH].set(w_in0[:, o_ab + NH:o_ab + 2 * NH])
    proj1 = _matmul(u, w_in0, 4 * QKW, 1024, "in_proj_qkvz")
    proj_ab = _matmul(u, w_ab, 256, 256, "in_proj_ab")
    proj3 = _matmul_shifted(u, w_in0, o_ab, 2 * NH, (PW + 2 * D) // 1024, PW // 1024, 1024, "in_proj_gates_pool")

    cw = conv_w[0]
    alog = _pad_lanes(a_log[0])
    dtb = _pad_lanes(dt_bias[0])

    t_p, l2_p, vb_p, kbd_p, qd_p, cd_p = _prep_prompt(proj1, proj_ab, cw, alog, dtb)
    o_p, s_p = _scan_prompt(t_p, l2_p, vb_p, kbd_p, qd_p, cd_p)

    qkv_s = _to_batch_major(proj1[RP:, :CONVC])
    conv_in = jnp.concatenate([state_conv[0], qkv_s, jnp.zeros((BS, SPAD - TS - (CONVW - 1), CONVC), F32)], axis=1)
    ab_s = _to_batch_major(proj_ab[RP:])
    ab_s = jnp.concatenate([ab_s, jnp.zeros((BS, SPAD - TS, 256), F32)], axis=1)
    m_s, l2_s, vb_s, kbd_s, qd_s, cd_s = _prep_sample(
        conv_in.reshape(BS * SPAD, CONVC), ab_s.reshape(BS * SPAD, 256), cw, alog, dtb)
    nblk = BS // SSEQ
    m_blocks = jnp.diagonal(m_s.reshape(nblk * NH, SSEQ, SPAD, SSEQ, SPAD), axis1=1, axis2=3)
    m_blocks = jnp.transpose(m_blocks, (1, 2, 0, 3)).reshape(SPAD, SPAD, nblk * NH * SSEQ)
    t_blocks = _solve(m_blocks, "tri_solve_sample").reshape(SPAD, SPAD, nblk * NH, SSEQ)
    t_s = jnp.einsum("ijgb,bc->gbicj", t_blocks, jnp.eye(SSEQ, dtype=F32))
    t_s = t_s.reshape(nblk * NH, SSEQ * SPAD, SSEQ * SPAD).astype(BF16)
    o_s8, s_s = _scan_sample(t_s, l2_s, vb_s, kbd_s, qd_s, cd_s, state_delta[0])
    o_s = _to_time_major(o_s8.reshape(BS, SPAD, QKW)[:, :TS])

    hist_s = jnp.transpose(state_pool[0], (1, 0, 2)).reshape(PHIST * BS, PW)
    merged = _branch_merge(o_p, o_s, proj1, proj3, hist_s, o_norm_g[0].reshape(1, DH), w_proj_a[0].astype(BF16),
                           pool_w[0].astype(BF16), pool_scale[0].reshape(1, PW), w_proj_b[0].astype(BF16))
    x1, h = _post_out(merged, xp2, xs2, mod_p, mod_s, norm2_g[0].reshape(1, D), w_out[0].astype(BF16))
    act = _ffn1(h, w_gate_up[0])
    y_p, y_s = _ffn2(act, w_down[0].astype(BF16), x1, mod_p, mod_s, final_g.reshape(1, D))

    xpool_s = _to_batch_major(proj3[RP:, 2 * D:])

    def tail_rows(a, n, c0, c1):
        return jnp.stack([a[(b + 1) * TP - n:(b + 1) * TP, c0:c1] for b in range(BP)])

    return (y_p.reshape(BP, TP, D),
            _to_batch_major(y_s),
            s_p[None],
            tail_rows(proj1, CONVW - 1, 0, CONVC)[None],
            tail_rows(proj3, PHIST, 2 * D, 2 * D + PW)[None],
            s_s[None],
            conv_in[:, TS:TS + CONVW - 1][None],
            jnp.concatenate([state_pool[0][:, TS:], xpool_s], axis=1)[None])
```

```python
import functools

import jax
import jax.numpy as jnp
from jax import lax
from jax.experimental import pallas as pl
from jax.experimental.pallas import tpu as pltpu

F32 = jnp.float32
BF16 = jnp.bfloat16

D = 2048
NH = 8
DH = 128
QKW = NH * DH
CONVC = 3 * QKW
CONVW = 4
PW = 1024
PGRP = 256
POOL_WINDOWS = (2, 4, 8, 16)
PHIST = 15
DFF = 5632
EPS = 1e-6

BP, TP = 4, 2048
BS, TS = 128, 4
RP = BP * TP
RS = BS * TS
R = RP + RS
TM = 512
NPT = RP // TM
TPB = TP // TM
TMM = R // 8
CHUNK = 64
SPAD = 8
SSEQ = 16

VMEM_LIMIT = 56 * 1024 * 1024

NT_DIMS = (((1,), (1,)), ((), ()))


def _params(*sem):
    return pltpu.CompilerParams(dimension_semantics=sem, vmem_limit_bytes=VMEM_LIMIT)


def _sigmoid(x):
    return 1.0 / (1.0 + jnp.exp(-x))


def _silu(x):
    return x * _sigmoid(x)


def _softplus(x):
    return jnp.maximum(x, 0.0) + jnp.log1p(jnp.exp(-jnp.abs(x)))


def _rms(x, gain):
    return x * lax.rsqrt(jnp.mean(x * x, axis=-1, keepdims=True) + EPS) * gain


def _dot(a, b):
    return jnp.dot(a, b, preferred_element_type=F32)


def _mod_kernel(c_ref, w_ref, b_ref, o_ref):
    c = c_ref[...]
    o_ref[...] = _dot(_silu(c).astype(BF16), w_ref[...].astype(BF16)) + b_ref[...]


def _mod(c_all, w_ada, b_ada):
    n = c_all.shape[0]
    tn = 1024
    return pl.pallas_call(
        _mod_kernel,
        grid=(6 * D // tn,),
        in_specs=[pl.BlockSpec((n, D), lambda j: (0, 0)),
                  pl.BlockSpec((D, tn), lambda j: (0, j)),
                  pl.BlockSpec((1, tn), lambda j: (0, j))],
        out_specs=pl.BlockSpec((n, tn), lambda j: (0, j)),
        out_shape=jax.ShapeDtypeStruct((n, 6 * D), F32),
        compiler_params=_params("arbitrary"),
        name="ada_mod",
    )(c_all, w_ada, b_ada)


def _mod_specs(col, tiles_per_seq):
    p = pl.BlockSpec((1, 1, D), lambda i, *_: (jnp.minimum(i // tiles_per_seq, BP - 1), 0, col))
    s = pl.BlockSpec((BS, D), lambda i, *_: (0, col))
    return p, s


def _pre_kernel(xp_ref, xs_ref, shp_ref, scp_ref, shs_ref, scs_ref, g_ref, u_ref):
    i = pl.program_id(0)
    g = g_ref[...]

    @pl.when(i < NPT)
    def _():
        u = _rms(xp_ref[...], g) * (1.0 + scp_ref[0]) + shp_ref[0]
        u_ref[...] = u.astype(BF16)

    @pl.when(i == NPT)
    def _():
        for t in range(TS):
            rows = slice(t * BS, (t + 1) * BS)
            u = _rms(xs_ref[rows, :], g) * (1.0 + scs_ref[...]) + shs_ref[...]
            u_ref[rows, :] = u.astype(BF16)


def _pre(xp2, xs2, mod_p, mod_s, g):
    shp, shs = _mod_specs(0, TPB)
    scp, scs = _mod_specs(1, TPB)
    return pl.pallas_call(
        _pre_kernel,
        grid=(NPT + 1,),
        in_specs=[pl.BlockSpec((TM, D), lambda i: (jnp.minimum(i, NPT - 1), 0)),
                  pl.BlockSpec((TM, D), lambda i: (0, 0)),
                  shp, scp, shs, scs,
                  pl.BlockSpec((1, D), lambda i: (0, 0))],
        out_specs=pl.BlockSpec((TM, D), lambda i: (i, 0)),
        out_shape=jax.ShapeDtypeStruct((R, D), BF16),
        compiler_params=_params("arbitrary"),
        name="norm1_mod",
    )(xp2, xs2, mod_p, mod_p, mod_s, mod_s, g)


def _mm_kernel(a_ref, w_ref, o_ref, wb_ref):
    @pl.when(pl.program_id(1) == 0)
    def _():
        wb_ref[...] = w_ref[...].astype(BF16)

    o_ref[...] = _dot(a_ref[...], wb_ref[...])


def _matmul(a, w, n_cols, tn, name):
    rows, k = a.shape
    return pl.pallas_call(
        _mm_kernel,
        grid=(n_cols // tn, rows // TMM),
        in_specs=[pl.BlockSpec((TMM, k), lambda j, i: (i, 0)),
                  pl.BlockSpec((k, tn), lambda j, i: (0, j))],
        out_specs=pl.BlockSpec((TMM, tn), lambda j, i: (i, j)),
        out_shape=jax.ShapeDtypeStruct((rows, n_cols), F32),
        scratch_shapes=[pltpu.VMEM((k, tn), BF16)],
        compiler_params=_params("arbitrary", "arbitrary"),
        name=name,
    )(a, w)


def _mm_nt_kernel(a_ref, w_ref, o_ref, wb_ref):
    @pl.when(pl.program_id(1) == 0)
    def _():
        wb_ref[...] = w_ref[...].astype(BF16)

    o_ref[...] = lax.dot_general(a_ref[...], wb_ref[...], NT_DIMS, preferred_element_type=F32)


def _matmul_nt(a, wt, n_cols, tn, name):
    rows, k = a.shape
    return pl.pallas_call(
        _mm_nt_kernel,
        grid=(n_cols // tn, rows // TMM),
        in_specs=[pl.BlockSpec((TMM, k), lambda j, i: (i, 0)),
                  pl.BlockSpec((tn, k), lambda j, i: (j, 0))],
        out_specs=pl.BlockSpec((TMM, tn), lambda j, i: (i, j)),
        out_shape=jax.ShapeDtypeStruct((rows, n_cols), F32),
        scratch_shapes=[pltpu.VMEM((tn, k), BF16)],
        compiler_params=_params("arbitrary", "arbitrary"),
        name=name,
    )(a, wt)


def _mm_nt_shift_kernel(shift, a_ref, w0_ref, w1_ref, o_ref, wb_ref):
    @pl.when(pl.program_id(1) == 0)
    def _():
        w = jnp.concatenate([w0_ref[shift:, :], w1_ref[...]], axis=0)
        wb_ref[...] = w.astype(BF16)

    o_ref[...] = lax.dot_general(a_ref[...], wb_ref[...], NT_DIMS, preferred_element_type=F32)


def _matmul_nt_shifted(a, wt, row0, shift, nt, rot, tn, name):
    rows, k = a.shape
    src = lambda j: lax.rem(j + rot, nt)
    return pl.pallas_call(
        functools.partial(_mm_nt_shift_kernel, shift),
        grid=(nt, rows // TMM),
        in_specs=[pl.BlockSpec((TMM, k), lambda j, i: (i, 0)),
                  pl.BlockSpec((tn, k), lambda j, i: (row0 // tn + src(j), 0)),
                  pl.BlockSpec((shift, k), lambda j, i: ((row0 + (src(j) + 1) * tn) // shift, 0))],
        out_specs=pl.BlockSpec((TMM, tn), lambda j, i: (i, j)),
        out_shape=jax.ShapeDtypeStruct((rows, nt * tn), F32),
        scratch_shapes=[pltpu.VMEM((tn, k), BF16)],
        compiler_params=_params("arbitrary", "arbitrary"),
        name=name,
    )(a, wt, wt)


def _delta_prep(get_qkv, ab, alog, dtb, valid, chunk, seq_block, put_m, l2_ref, vb_ref, kbd_ref, qd_ref, cd_ref):
    rows = ab.shape[0]
    a = ab[:, :128]
    b = ab[:, 128:]
    beta = _sigmoid(b)
    g = -jnp.exp(alog) * _softplus(a + dtb)
    if valid is not None:
        beta = jnp.where(valid, beta, 0.0)
        g = jnp.where(valid, g, 0.0)
    shift = seq_block.bit_length() - 1
    r = lax.broadcasted_iota(jnp.int32, (rows, rows), 0)
    c = lax.broadcasted_iota(jnp.int32, (rows, rows), 1)
    same = (r >> shift) == (c >> shift)
    ltri = jnp.where(same & (r >= c), 1.0, 0.0)
    lall = jnp.where(same, 1.0, 0.0)
    dcol = jnp.dot(ltri, g, preferred_element_type=F32, precision=lax.Precision.HIGHEST)
    last = jnp.dot(lall, g, preferred_element_type=F32, precision=lax.Precision.HIGHEST)
    cd_ref[...] = jnp.exp(last)
    drow = dcol.T
    edc = jnp.exp(dcol)
    etl = jnp.exp(last - dcol)
    rc = lax.broadcasted_iota(jnp.int32, (chunk, chunk), 0)
    cc = lax.broadcasted_iota(jnp.int32, (chunk, chunk), 1)
    same_c = (rc >> shift) == (cc >> shift)
    causal = same_c & (rc >= cc)
    strict = same_c & (rc > cc)
    for h in range(NH):
        cols = slice(h * DH, (h + 1) * DH)
        for sc in range(rows // chunk):
            rs = slice(sc * chunk, (sc + 1) * chunk)
            q, k, v = get_qkv(h, sc)
            qn = q * lax.rsqrt(jnp.sum(q * q, axis=-1, keepdims=True) + EPS) * (DH ** -0.5)
            kn = k * lax.rsqrt(jnp.sum(k * k, axis=-1, keepdims=True) + EPS)
            bcol = beta[rs, h:h + 1]
            ed = edc[rs, h:h + 1]
            kb = kn * bcol
            vb_ref[rs, cols] = (v * bcol).astype(BF16)
            kbd_ref[rs, cols] = (kb * ed).astype(BF16)
            qd_ref[rs, cols] = (qn * ed).astype(BF16)
            kt = kn * etl[rs, h:h + 1]
            diff = dcol[rs, h:h + 1] - drow[h:h + 1, rs]
            lm = jnp.where(causal, jnp.exp(jnp.where(causal, diff, 0.0)), 0.0)
            lhs = jnp.concatenate([kb, qn], axis=0).astype(BF16)
            s = lax.dot_general(lhs, kn.astype(BF16), NT_DIMS, preferred_element_type=F32)
            idx = sc * NH + h
            put_m(idx, jnp.where(strict, s[:chunk] * lm, 0.0))
            l2_ref[idx] = jnp.concatenate([s[chunk:] * lm, kt.T], axis=0).astype(BF16)


LANES = 128
CPB = TM // CHUNK * NH
assert 2 * CPB == LANES and TPB % 2 == 0


def _solve_lanes(mt_ref, xt_ref):
    sub = lax.broadcasted_iota(jnp.int32, (8, LANES), 0)
    zero = jnp.zeros((8, LANES), F32)
    for i in range(CHUNK):
        groups = i // 8 + 1
        acc = [jnp.where(sub + 8 * v == i, 1.0, 0.0) for v in range(groups)]
        for j in range(i):
            m = mt_ref[i, j:j + 1, :]
            for v in range(j // 8 + 1):
                acc[v] = acc[v] - m * xt_ref[j, 8 * v:8 * v + 8, :]
        for v in range(CHUNK // 8):
            xt_ref[i, 8 * v:8 * v + 8, :] = acc[v] if v < groups else zero


def _prep_prompt_kernel(cur_ref, prev_ref, ab_ref, cw_ref, alog_ref, dtb_ref,
                        t_ref, l2_ref, vb_ref, kbd_ref, qd_ref, cd_ref, xh_ref, mall_ref, mt_ref, xt_ref):
    blk = pl.program_id(1)
    odd = blk % 2
    xh_ref[0:8, :] = jnp.where(blk > 0, prev_ref[...], 0.0)
    xh_ref[8:8 + CHUNK, :] = cur_ref[0:CHUNK, :]
    w = cw_ref[...]

    def conv(sc, cols):
        src, base = (xh_ref, 8) if sc == 0 else (cur_ref, sc * CHUNK)
        y = src[base:base + CHUNK, cols] * w[CONVW - 1:CONVW, cols]
        for j in range(CONVW - 1):
            lo = base - (CONVW - 1) + j
            y = y + src[lo:lo + CHUNK, cols] * w[j:j + 1, cols]
        return _silu(y)

    def get_qkv(h, sc):
        return tuple(conv(sc, slice(p * QKW + h * DH, p * QKW + (h + 1) * DH)) for p in range(3))

    def put_m(idx, m):
        start = pl.multiple_of((odd * CPB + idx) * CHUNK, CHUNK)
        mall_ref[pl.ds(start, CHUNK), :] = m

    _delta_prep(get_qkv, ab_ref[...], alog_ref[...], dtb_ref[...], None, CHUNK, CHUNK,
                put_m, l2_ref, vb_ref, kbd_ref, qd_ref, cd_ref)

    @pl.when(odd == 1)
    def _():
        for i in range(CHUNK):
            mt_ref[i] = mall_ref[pl.ds(i, LANES, stride=CHUNK), :].T
        _solve_lanes(mt_ref, xt_ref)
        for i in range(CHUNK):
            mall_ref[pl.ds(i, LANES, stride=CHUNK), :] = xt_ref[i].T
        t_ref[...] = mall_ref[...].astype(BF16)


def _prep_prompt(proj1, proj_ab, conv_w, alog, dtb):
    nch = RP // CHUNK * NH
    row = lambda b, k: b * TPB + k
    tok = pl.BlockSpec((TM, QKW), lambda b, k: (row(b, k), 0))
    t, *rest = pl.pallas_call(
        _prep_prompt_kernel,
        grid=(BP, TPB),
        in_specs=[pl.BlockSpec((TM, CONVC), lambda b, k: (row(b, k), 0)),
                  pl.BlockSpec((8, CONVC), lambda b, k: (jnp.maximum(row(b, k) * (TM // 8) - 1, 0), 0)),
                  pl.BlockSpec((TM, 256), lambda b, k: (row(b, k), 0)),
                  pl.BlockSpec((CONVW, CONVC), lambda b, k: (0, 0)),
                  pl.BlockSpec((1, 128), lambda b, k: (0, 0)),
                  pl.BlockSpec((1, 128), lambda b, k: (0, 0))],
        out_specs=[pl.BlockSpec((LANES * CHUNK, CHUNK), lambda b, k: (row(b, k) // 2, 0)),
                   pl.BlockSpec((CPB, CHUNK + DH, CHUNK), lambda b, k: (row(b, k), 0, 0)),
                   tok, tok, tok,
                   pl.BlockSpec((TM, 128), lambda b, k: (row(b, k), 0))],
        out_shape=[jax.ShapeDtypeStruct((nch * CHUNK, CHUNK), BF16),
                   jax.ShapeDtypeStruct((nch, CHUNK + DH, CHUNK), BF16),
                   jax.ShapeDtypeStruct((RP, QKW), BF16),
                   jax.ShapeDtypeStruct((RP, QKW), BF16),
                   jax.ShapeDtypeStruct((RP, QKW), BF16),
                   jax.ShapeDtypeStruct((RP, 128), F32)],
        scratch_shapes=[pltpu.VMEM((8 + CHUNK, CONVC), F32),
                        pltpu.VMEM((LANES * CHUNK, CHUNK), F32),
                        pltpu.VMEM((CHUNK, CHUNK, LANES), F32),
                        pltpu.VMEM((CHUNK, CHUNK, LANES), F32)],
        compiler_params=_params("arbitrary", "arbitrary"),
        name="delta_prep_prompt",
    )(proj1, proj1, proj_ab, conv_w, alog, dtb)
    return (t.reshape(nch, CHUNK, CHUNK), *rest)


def _prep_sample_kernel(xp_ref, ab_ref, cw_ref, alog_ref, dtb_ref,
                        m_ref, l2_ref, vb_ref, kbd_ref, qd_ref, cd_ref):
    nrow = SSEQ * SPAD
    w = cw_ref[...]
    valid = (lax.broadcasted_iota(jnp.int32, (nrow, 1), 0) & (SPAD - 1)) < TS

    def conv(cols):
        x = xp_ref[:, cols]
        y = x * w[0:1, cols]
        for j in range(1, CONVW):
            y = y + pltpu.roll(x, nrow - j, axis=0) * w[j:j + 1, cols]
        return jnp.where(valid, _silu(y), 0.0)

    def get_qkv(h, sc):
        del sc
        return tuple(conv(slice(p * QKW + h * DH, p * QKW + (h + 1) * DH)) for p in range(3))

    def put_m(idx, m):
        m_ref[idx] = m

    _delta_prep(get_qkv, ab_ref[...], alog_ref[...], dtb_ref[...], valid, nrow, SPAD,
                put_m, l2_ref, vb_ref, kbd_ref, qd_ref, cd_ref)


def _prep_sample(xp8, ab8, conv_w, alog, dtb):
    nrow = SSEQ * SPAD
    steps = BS // SSEQ
    tok = pl.BlockSpec((nrow, QKW), lambda s: (s, 0))
    mat = pl.BlockSpec((NH, nrow, nrow), lambda s: (s, 0, 0))
    return pl.pallas_call(
        _prep_sample_kernel,
        grid=(steps,),
        in_specs=[pl.BlockSpec((nrow, CONVC), lambda s: (s, 0)),
                  pl.BlockSpec((nrow, 256), lambda s: (s, 0)),
                  pl.BlockSpec((CONVW, CONVC), lambda s: (0, 0)),
                  pl.BlockSpec((1, 128), lambda s: (0, 0)),
                  pl.BlockSpec((1, 128), lambda s: (0, 0))],
        out_specs=[mat, pl.BlockSpec((NH, nrow + DH, nrow), lambda s: (s, 0, 0)),
                   tok, tok, tok, pl.BlockSpec((nrow, 128), lambda s: (s, 0))],
        out_shape=[jax.ShapeDtypeStruct((steps * NH, nrow, nrow), F32),
                   jax.ShapeDtypeStruct((steps * NH, nrow + DH, nrow), BF16),
                   jax.ShapeDtypeStruct((BS * SPAD, QKW), BF16),
                   jax.ShapeDtypeStruct((BS * SPAD, QKW), BF16),
                   jax.ShapeDtypeStruct((BS * SPAD, QKW), BF16),
                   jax.ShapeDtypeStruct((BS * SPAD, 128), F32)],
        compiler_params=_params("arbitrary"),
        name="delta_prep_sample",
    )(xp8, ab8, conv_w, alog, dtb)


def _solve_kernel(m_ref, x_ref):
    n, _, lanes = m_ref.shape
    rowid = lax.broadcasted_iota(jnp.int32, (n, lanes), 0)

    def body_i(i, carry):
        def body_j(j, acc):
            return acc - m_ref[i, pl.ds(j, 1), :] * x_ref[j]

        x_ref[i] = lax.fori_loop(0, i, body_j, (rowid == i).astype(F32))
        return carry

    lax.fori_loop(0, n, body_i, 0)


def _solve(m, name):
    n, _, g = m.shape
    lanes = 128
    spec = pl.BlockSpec((n, n, lanes), lambda s: (0, 0, s))
    return pl.pallas_call(
        _solve_kernel,
        grid=(g // lanes,),
        in_specs=[spec],
        out_specs=spec,
        out_shape=jax.ShapeDtypeStruct((n, n, g), F32),
        compiler_params=_params("arbitrary"),
        name=name,
    )(m)


CPS = 2


def _scan_prompt_kernel(t_ref, l2_ref, vb_ref, kbd_ref, qd_ref, cd_ref, o_ref, sout_ref, s_ref, ub_ref, wd_ref):
    step = pl.program_id(0)

    @pl.when(step == 0)
    def _():
        s_ref[...] = jnp.zeros_like(s_ref)

    chains = [(b, h) for b in range(BP) for h in range(NH)]
    for c in range(CPS):
        rows = slice(c * CHUNK, (c + 1) * CHUNK)
        for n, (b, h) in enumerate(chains):
            cols = slice(h * DH, (h + 1) * DH)
            rhs = jnp.concatenate([vb_ref[b, rows, cols], kbd_ref[b, rows, cols]], axis=1)
            r1 = _dot(t_ref[b, c * NH + h], rhs)
            ub_ref[c * len(chains) + n] = r1[:, :DH]
            wd_ref[c * len(chains) + n] = r1[:, DH:].astype(BF16)
    for c in range(CPS):
        rows = slice(c * CHUNK, (c + 1) * CHUNK)
        r2s = []
        for n, (b, h) in enumerate(chains):
            cols = slice(h * DH, (h + 1) * DH)
            lhs = jnp.concatenate([wd_ref[c * len(chains) + n], qd_ref[b, rows, cols]], axis=0)
            r2s.append(_dot(lhs, s_ref[n].astype(BF16)))
        for n, (b, h) in enumerate(chains):
            cols = slice(h * DH, (h + 1) * DH)
            r2 = r2s[n]
            u = (ub_ref[c * len(chains) + n] - r2[:CHUNK]).astype(BF16)
            r3 = _dot(l2_ref[b, c * NH + h], u)
            o_ref[b, rows, cols] = r2[CHUNK:] + r3[:CHUNK]
            cd = cd_ref[b, c * CHUNK:c * CHUNK + 1, h:h + 1]
            s_ref[n] = s_ref[n] * cd + r3[CHUNK:]

    @pl.when(step == pl.num_programs(0) - 1)
    def _():
        sout_ref[...] = s_ref[...]


def _scan_prompt(t, l2, vb, kbd, qd, cd):
    nchunk = TP // CHUNK
    rows = CPS * CHUNK
    t = t.reshape(BP, nchunk * NH, CHUNK, CHUNK)
    l2 = l2.reshape(BP, nchunk * NH, CHUNK + DH, CHUNK)
    tok3 = lambda a: a.reshape(BP, TP, a.shape[-1])
    tok = pl.BlockSpec((BP, rows, QKW), lambda s: (0, s, 0))
    o, s_fin = pl.pallas_call(
        _scan_prompt_kernel,
        grid=(nchunk // CPS,),
        in_specs=[pl.BlockSpec((BP, CPS * NH, CHUNK, CHUNK), lambda s: (0, s, 0, 0)),
                  pl.BlockSpec((BP, CPS * NH, CHUNK + DH, CHUNK), lambda s: (0, s, 0, 0)),
                  tok, tok, tok,
                  pl.BlockSpec((BP, rows, 128), lambda s: (0, s, 0))],
        out_specs=[tok, pl.BlockSpec((BP * NH, DH, DH), lambda s: (0, 0, 0))],
        out_shape=[jax.ShapeDtypeStruct((BP, TP, QKW), F32),
                   jax.ShapeDtypeStruct((BP * NH, DH, DH), F32)],
        scratch_shapes=[pltpu.VMEM((BP * NH, DH, DH), F32),
                        pltpu.VMEM((CPS * BP * NH, CHUNK, DH), F32),
                        pltpu.VMEM((CPS * BP * NH, CHUNK, DH), BF16)],
        compiler_params=_params("arbitrary"),
        name="delta_scan_prompt",
    )(t, l2, tok3(vb), tok3(kbd), tok3(qd), tok3(cd))
    return o.reshape(RP, QKW), s_fin.reshape(BP, NH, DH, DH)


def _scan_sample_kernel(t_ref, l2_ref, vb_ref, kbd_ref, qd_ref, cd_ref, s0_ref, o_ref, sout_ref):
    nrow = SSEQ * SPAD
    seq_of_row = lax.broadcasted_iota(jnp.int32, (nrow, 1), 0) >> (SPAD.bit_length() - 1)
    for h in range(NH):
        cols = slice(h * DH, (h + 1) * DH)
        rhs = jnp.concatenate([vb_ref[:, cols], kbd_ref[:, cols]], axis=1)
        r1 = _dot(t_ref[h], rhs)
        qd = qd_ref[:, cols].astype(F32)
        u_parts, qs_parts = [], []
        for b in range(SSEQ):
            rows = slice(b * SPAD, (b + 1) * SPAD)
            lhs = jnp.concatenate([r1[rows, DH:], qd[rows]], axis=0).astype(BF16)
            r2 = _dot(lhs, s0_ref[b, h].astype(BF16))
            u_parts.append(r1[rows, :DH] - r2[:SPAD])
            qs_parts.append(r2[SPAD:])
        u = jnp.concatenate(u_parts, axis=0)
        o_ref[:, cols] = jnp.concatenate(qs_parts, axis=0) + _dot(l2_ref[h, :nrow, :], u.astype(BF16))
        ktt = l2_ref[h, nrow:, :]
        for b in range(SSEQ):
            ub = jnp.where(seq_of_row == b, u, 0.0).astype(BF16)
            cd = cd_ref[b * SPAD:b * SPAD + 1, h:h + 1]
            sout_ref[b, h] = s0_ref[b, h] * cd + _dot(ktt, ub)


def _scan_sample(t, l2, vb, kbd, qd, cd, s0):
    nrow = SSEQ * SPAD
    steps = BS // SSEQ
    tok = pl.BlockSpec((nrow, QKW), lambda s: (s, 0))
    st = pl.BlockSpec((SSEQ, NH, DH, DH), lambda s: (s, 0, 0, 0))
    return pl.pallas_call(
        _scan_sample_kernel,
        grid=(steps,),
        in_specs=[pl.BlockSpec((NH, nrow, nrow), lambda s: (s, 0, 0)),
                  pl.BlockSpec((NH, nrow + DH, nrow), lambda s: (s, 0, 0)),
                  tok, tok, tok, pl.BlockSpec((nrow, 128), lambda s: (s, 0)), st],
        out_specs=[tok, st],
        out_shape=[jax.ShapeDtypeStruct((BS * SPAD, QKW), F32),
                   jax.ShapeDtypeStruct((BS, NH, DH, DH), F32)],
        compiler_params=_params("arbitrary"),
        name="delta_scan_sample",
    )(t, l2, vb, kbd, qd, cd, s0)


TO = 256
NPO = RP // TO
TPO = TP // TO
POOL_PREV = 16


def _resident(shape):
    return pl.BlockSpec(shape, lambda i: (0,) * len(shape), pipeline_mode=pl.Buffered(1))


def _branch_a_act(o_ref, z_ref, og):
    parts = []
    for h in range(NH):
        cols = slice(h * DH, (h + 1) * DH)
        parts.append((_rms(o_ref[:, cols], og) * _silu(z_ref[:, cols])).astype(BF16))
    return jnp.concatenate(parts, axis=1)


def _branch_b_proj(yp, pw_ref, ps_ref, wb_ref):
    parts = [_dot(y.astype(BF16), pw_ref[gi]) for gi, y in enumerate(yp)]
    y = jnp.concatenate(parts, axis=1) * ps_ref[...]
    return _dot(y.astype(BF16), wb_ref[...])


def _branch_merge_prompt_kernel(o_ref, z_ref, x_ref, prev_ref, ga_ref, gb_ref,
                                og_ref, wa_ref, pw_ref, ps_ref, wb_ref, m_ref):
    k = pl.program_id(0) % TPO
    ya = _dot(_branch_a_act(o_ref, z_ref, og_ref[...]), wa_ref[...])

    x = x_ref[...]
    v = jnp.concatenate([jnp.where(k > 0, prev_ref[...], 0.0), x], axis=0)
    pos = k * TO + lax.broadcasted_iota(jnp.int32, (TO, 1), 0)
    yp = []
    for gi, win in enumerate(POOL_WINDOWS):
        cols = slice(gi * PGRP, (gi + 1) * PGRP)
        s = v[:, cols]
        shift = 1
        while shift < win:
            s = s + pltpu.roll(s, shift, axis=0)
            shift *= 2
        cnt = jnp.minimum(win, pos + 1).astype(F32)
        yp.append(s[POOL_PREV:] / cnt - x[:, cols])
    yb = _branch_b_proj(yp, pw_ref, ps_ref, wb_ref)

    m_ref[...] = (_sigmoid(ga_ref[...]) * ya + _sigmoid(gb_ref[...]) * yb).astype(BF16)


def _branch_merge_sample_kernel(o_ref, z_ref, x_ref, hist_ref, ga_ref, gb_ref,
                                og_ref, wa_ref, pw_ref, ps_ref, wb_ref, buf_ref, m_ref):
    del buf_ref
    ya = _dot(_branch_a_act(o_ref, z_ref, og_ref[...]), wa_ref[...])

    def slab(src, cols):
        if src >= PHIST:
            return x_ref[(src - PHIST) * BS:(src - PHIST + 1) * BS, cols]
        return hist_ref[src * BS:(src + 1) * BS, cols]

    yp = []
    for gi, win in enumerate(POOL_WINDOWS):
        cols = slice(gi * PGRP, (gi + 1) * PGRP)
        slabs = []
        for t in range(TS):
            acc = slab(PHIST + t, cols)
            for d in range(1, win):
                acc = acc + slab(PHIST + t - d, cols)
            slabs.append(acc / float(win) - slab(PHIST + t, cols))
        yp.append(jnp.concatenate(slabs, axis=0))
    yb = _branch_b_proj(yp, pw_ref, ps_ref, wb_ref)

    m_ref[...] = (_sigmoid(ga_ref[...]) * ya + _sigmoid(gb_ref[...]) * yb).astype(BF16)


def _branch_merge(o_p, o_s, proj1, proj3, hist_s, o_norm_g, w_proj_a, pool_w, pool_scale, w_proj_b):
    assert all(w & (w - 1) == 0 and w <= POOL_PREV for w in POOL_WINDOWS)
    xcol = 2 * D // PW
    weights = [_resident((1, DH)), _resident((QKW, D)), _resident((len(POOL_WINDOWS), PGRP, PGRP)),
               _resident((1, PW)), _resident((PW, D))]
    wargs = (o_norm_g, w_proj_a, pool_w, pool_scale, w_proj_b)
    merged = pl.pallas_call(
        _branch_merge_prompt_kernel,
        grid=(NPO,),
        in_specs=[pl.BlockSpec((TO, QKW), lambda i: (i, 0)),
                  pl.BlockSpec((TO, QKW), lambda i: (i, 3)),
                  pl.BlockSpec((TO, PW), lambda i: (i, xcol)),
                  pl.BlockSpec((POOL_PREV, PW), lambda i: (jnp.maximum(i * (TO // POOL_PREV) - 1, 0), xcol)),
                  pl.BlockSpec((TO, D), lambda i: (i, 0)),
                  pl.BlockSpec((TO, D), lambda i: (i, 1))] + weights,
        out_specs=pl.BlockSpec((TO, D), lambda i: (i, 0)),
        out_shape=jax.ShapeDtypeStruct((R, D), BF16),
        compiler_params=_params("arbitrary"),
        name="branch_merge_prompt",
    )(o_p, proj1, proj3, proj3, proj3, proj3, *wargs)
    sblk = RP // RS
    return pl.pallas_call(
        _branch_merge_sample_kernel,
        grid=(1,),
        in_specs=[pl.BlockSpec((RS, QKW), lambda i: (0, 0)),
                  pl.BlockSpec((RS, QKW), lambda i: (sblk, 3)),
                  pl.BlockSpec((RS, PW), lambda i: (sblk, xcol)),
                  pl.BlockSpec((PHIST * BS, PW), lambda i: (0, 0)),
                  pl.BlockSpec((RS, D), lambda i: (sblk, 0)),
                  pl.BlockSpec((RS, D), lambda i: (sblk, 1))] + weights + [pl.BlockSpec(memory_space=pl.ANY)],
        out_specs=pl.BlockSpec((RS, D), lambda i: (sblk, 0)),
        out_shape=jax.ShapeDtypeStruct((R, D), BF16),
        input_output_aliases={11: 0},
        compiler_params=_params("arbitrary"),
        name="branch_merge_sample",
    )(o_s, proj1, proj3, hist_s, proj3, proj3, *wargs, merged)


NCH = 512


def _proj_residual(a_ref, w_ref, x_ref, gate, out_ref):
    ss = 0.0
    for n in range(D // NCH):
        cols = slice(n * NCH, (n + 1) * NCH)
        r = x_ref[:, cols] + gate[:, cols] * _dot(a_ref[...], w_ref[:, cols])
        out_ref[:, cols] = r
        ss = ss + jnp.sum(r * r, axis=-1, keepdims=True)
    return ss


def _tile_rows(v, rows):
    return jnp.concatenate([v] * (rows // BS), axis=0)


def _post_out_kernel(m_ref, xp_ref, xs_ref,
                     gtp_ref, shp_ref, scp_ref, gts_ref, shs_ref, scs_ref, g2_ref, w_ref,
                     x1_ref, h_ref):
    i = pl.program_id(0)

    def body(x_ref, gt, sh, sc):
        ss = _proj_residual(m_ref, w_ref, x_ref, gt, x1_ref)
        inv = lax.rsqrt(ss * (1.0 / D) + EPS)
        h_ref[...] = (x1_ref[...] * inv * g2_ref[...] * (1.0 + sc) + sh).astype(BF16)

    @pl.when(i < NPT)
    def _():
        body(xp_ref, gtp_ref[0], shp_ref[0], scp_ref[0])

    @pl.when(i == NPT)
    def _():
        body(xs_ref, _tile_rows(gts_ref[...], TM), _tile_rows(shs_ref[...], TM), _tile_rows(scs_ref[...], TM))


def _post_out(merged, xp2, xs2, mod_p, mod_s, norm2_g, w_out):
    gtp, gts = _mod_specs(2, TPB)
    shp, shs = _mod_specs(3, TPB)
    scp, scs = _mod_specs(4, TPB)
    tile = pl.BlockSpec((TM, D), lambda i: (i, 0))
    return pl.pallas_call(
        _post_out_kernel,
        grid=(NPT + 1,),
        in_specs=[tile,
                  pl.BlockSpec((TM, D), lambda i: (jnp.minimum(i, NPT - 1), 0)),
                  _resident((TM, D)),
                  gtp, shp, scp, gts, shs, scs,
                  _resident((1, D)),
                  _resident((D, D))],
        out_specs=[tile, tile],
        out_shape=[jax.ShapeDtypeStruct((R, D), F32), jax.ShapeDtypeStruct((R, D), BF16)],
        compiler_params=_params("arbitrary"),
        name="out_proj_norm2",
    )(merged, xp2, xs2, mod_p, mod_p, mod_p, mod_s, mod_s, mod_s, norm2_g, w_out)


FF_TN = 512


def _ffn1_kernel(h_ref, wg_ref, wu_ref, a_ref, wgb_ref, wub_ref):
    @pl.when(pl.program_id(1) == 0)
    def _():
        wgb_ref[...] = wg_ref[...].astype(BF16)
        wub_ref[...] = wu_ref[...].astype(BF16)

    h = h_ref[...]
    a_ref[...] = (_silu(_dot(h, wgb_ref[...])) * _dot(h, wub_ref[...])).astype(BF16)


def _ffn1(h, w_gate_up):
    nj = DFF // FF_TN
    return pl.pallas_call(
        _ffn1_kernel,
        grid=(nj, R // TMM),
        in_specs=[pl.BlockSpec((TMM, D), lambda j, i: (i, 0)),
                  pl.BlockSpec((D, FF_TN), lambda j, i: (0, j)),
                  pl.BlockSpec((D, FF_TN), lambda j, i: (0, nj + j))],
        out_specs=pl.BlockSpec((TMM, FF_TN), lambda j, i: (i, j)),
        out_shape=jax.ShapeDtypeStruct((R, DFF), BF16),
        scratch_shapes=[pltpu.VMEM((D, FF_TN), BF16), pltpu.VMEM((D, FF_TN), BF16)],
        compiler_params=_params("arbitrary", "arbitrary"),
        name="ffn_gate_up",
    )(h, w_gate_up, w_gate_up)


def _ffn2_kernel(a_ref, w_ref, x1_ref, gtp_ref, gts_ref, fg_ref, yp_ref, ys_ref):
    i = pl.program_id(0)

    def body(gt, y_ref):
        ss = _proj_residual(a_ref, w_ref, x1_ref, gt, y_ref)
        y_ref[...] = y_ref[...] * lax.rsqrt(ss * (1.0 / D) + EPS) * fg_ref[...]

    @pl.when(i < NPO)
    def _():
        body(gtp_ref[0], yp_ref)

    @pl.when(i >= NPO)
    def _():
        body(_tile_rows(gts_ref[...], TO), ys_ref)


def _ffn2(act, w_down, x1, mod_p, mod_s, final_g):
    gtp, gts = _mod_specs(5, TPO)
    return pl.pallas_call(
        _ffn2_kernel,
        grid=(R // TO,),
        in_specs=[pl.BlockSpec((TO, DFF), lambda i: (i, 0)),
                  _resident((DFF, D)),
                  pl.BlockSpec((TO, D), lambda i: (i, 0)),
                  gtp, gts,
                  _resident((1, D))],
        out_specs=[pl.BlockSpec((TO, D), lambda i: (jnp.minimum(i, NPO - 1), 0)),
                   pl.BlockSpec((TO, D), lambda i: (jnp.maximum(i - NPO, 0), 0))],
        out_shape=[jax.ShapeDtypeStruct((RP, D), F32), jax.ShapeDtypeStruct((RS, D), F32)],
        compiler_params=_params("arbitrary"),
        name="ffn_down_final",
    )(act, w_down, x1, mod_p, mod_s, final_g)


def _to_time_major(x):
    return jnp.transpose(x, (1, 0, 2)).reshape(TS * BS, x.shape[-1])


def _to_batch_major(x):
    return jnp.transpose(x.reshape(TS, BS, x.shape[-1]), (1, 0, 2))


def _pad_lanes(v):
    return jnp.zeros((1, 128), F32).at[0, :NH].set(v.astype(F32))


def kernel(x_prompt, x_sample, c_prompt, c_sample, state_delta, state_conv, state_pool, w_ada, b_ada, norm1_g,
           w_in, conv_w, a_log, dt_bias, o_norm_g, pool_w, pool_scale, w_proj_a, w_proj_b, w_out, norm2_g,
           w_gate_up, w_down, final_g):
    assert w_ada.shape[0] == 1, "single layer"
    xp2 = x_prompt.reshape(RP, D)
    xs2 = _to_time_major(x_sample)

    c_all = jnp.concatenate([c_sample, c_prompt, jnp.zeros((4, D), F32)], axis=0)
    mod = _mod(c_all, w_ada[0], b_ada[0].reshape(1, 6 * D))
    mod_s = mod[:BS]
    mod_p = mod[BS:BS + BP].reshape(BP, 1, 6 * D)

    u = _pre(xp2, xs2, mod_p, mod_s, norm1_g[0].reshape(1, D))

    w_in0 = w_in[0]
    o_ab = 4 * QKW
    w_ab = jnp.zeros((D, 256), F32)
    w_ab = w_ab.at[:, :NH].set(w_in0[:, o_ab:o_ab + NH]).at[:, 128:128 + NH].set(w_in0[:, o_ab + NH:o_ab + 2 * NH])
    w_in_t = jnp.transpose(w_in0)
    proj1 = _matmul_nt(u, w_in_t, 4 * QKW, 1024, "in_proj_qkvz")
    proj_ab = _matmul(u, w_ab, 256, 256, "in_proj_ab")
    proj3 = _matmul_nt_shifted(u, w_in_t, o_ab, 2 * NH, (PW + 2 * D) // 1024, PW // 1024, 1024, "in_proj_gates_pool")

    cw = conv_w[0]
    alog = _pad_lanes(a_log[0])
    dtb = _pad_lanes(dt_bias[0])

    t_p, l2_p, vb_p, kbd_p, qd_p, cd_p = _prep_prompt(proj1, proj_ab, cw, alog, dtb)
    o_p, s_p = _scan_prompt(t_p, l2_p, vb_p, kbd_p, qd_p, cd_p)

    qkv_s = _to_batch_major(proj1[RP:, :CONVC])
    conv_in = jnp.concatenate([state_conv[0], qkv_s, jnp.zeros((BS, SPAD - TS - (CONVW - 1), CONVC), F32)], axis=1)
    ab_s = _to_batch_major(proj_ab[RP:])
    ab_s = jnp.concatenate([ab_s, jnp.zeros((BS, SPAD - TS, 256), F32)], axis=1)
    m_s, l2_s, vb_s, kbd_s, qd_s, cd_s = _prep_sample(
        conv_in.reshape(BS * SPAD, CONVC), ab_s.reshape(BS * SPAD, 256), cw, alog, dtb)
    nblk = BS // SSEQ
    m_blocks = jnp.diagonal(m_s.reshape(nblk * NH, SSEQ, SPAD, SSEQ, SPAD), axis1=1, axis2=3)
    m_blocks = jnp.transpose(m_blocks, (1, 2, 0, 3)).reshape(SPAD, SPAD, nblk * NH * SSEQ)
    t_blocks = _solve(m_blocks, "tri_solve_sample").reshape(SPAD, SPAD, nblk * NH, SSEQ)
    t_s = jnp.einsum("ijgb,bc->gbicj", t_blocks, jnp.eye(SSEQ, dtype=F32))
    t_s = t_s.reshape(nblk * NH, SSEQ * SPAD, SSEQ * SPAD).astype(BF16)
    o_s8, s_s = _scan_sample(t_s, l2_s, vb_s, kbd_s, qd_s, cd_s, state_delta[0])
    o_s = _to_time_major(o_s8.reshape(BS, SPAD, QKW)[:, :TS])

    hist_s = jnp.transpose(state_pool[0], (1, 0, 2)).reshape(PHIST * BS, PW)
    merged = _branch_merge(o_p, o_s, proj1, proj3, hist_s, o_norm_g[0].reshape(1, DH), w_proj_a[0].astype(BF16),
                           pool_w[0].astype(BF16), pool_scale[0].reshape(1, PW), w_proj_b[0].astype(BF16))
    x1, h = _post_out(merged, xp2, xs2, mod_p, mod_s, norm2_g[0].reshape(1, D), w_out[0].astype(BF16))
    act = _ffn1(h, w_gate_up[0])
    y_p, y_s = _ffn2(act, w_down[0].astype(BF16), x1, mod_p, mod_s, final_g.reshape(1, D))

    xpool_s = _to_batch_major(proj3[RP:, 2 * D:])

    def tail_rows(a, n, c0, c1):
        return jnp.stack([a[(b + 1) * TP - n:(b + 1) * TP, c0:c1] for b in range(BP)])

    return (y_p.reshape(BP, TP, D),
            _to_batch_major(y_s),
            s_p[None],
            tail_rows(proj1, CONVW - 1, 0, CONVC)[None],
            tail_rows(proj3, PHIST, 2 * D, 2 * D + PW)[None],
            s_s[None],
            conv_in[:, TS:TS + CONVW - 1][None],
            jnp.concatenate([state_pool[0][:, TS:], xpool_s], axis=1)[None])
```

```python
import functools

import jax
import jax.numpy as jnp
from jax import lax
from jax.experimental import pallas as pl
from jax.experimental.pallas import tpu as pltpu

F32 = jnp.float32
BF16 = jnp.bfloat16

D = 2048
NH = 8
DH = 128
QKW = NH * DH
CONVC = 3 * QKW
CONVW = 4
PW = 1024
PGRP = 256
POOL_WINDOWS = (2, 4, 8, 16)
PHIST = 15
DFF = 5632
EPS = 1e-6

BP, TP = 4, 2048
BS, TS = 128, 4
RP = BP * TP
RS = BS * TS
R = RP + RS
TM = 512
NPT = RP // TM
TPB = TP // TM
TMM = R // 8
CHUNK = 64
SPAD = 8
SSEQ = 16

VMEM_LIMIT = 56 * 1024 * 1024

NT_DIMS = (((1,), (1,)), ((), ()))


def _params(*sem):
    return pltpu.CompilerParams(dimension_semantics=sem, vmem_limit_bytes=VMEM_LIMIT)


def _sigmoid(x):
    return 1.0 / (1.0 + jnp.exp(-x))


def _silu(x):
    return x * _sigmoid(x)


def _softplus(x):
    return jnp.maximum(x, 0.0) + jnp.log1p(jnp.exp(-jnp.abs(x)))


def _rms(x, gain):
    return x * lax.rsqrt(jnp.mean(x * x, axis=-1, keepdims=True) + EPS) * gain


def _dot(a, b):
    return jnp.dot(a, b, preferred_element_type=F32)


def _mod_kernel(c_ref, w_ref, b_ref, o_ref):
    c = c_ref[...]
    o_ref[...] = _dot(_silu(c).astype(BF16), w_ref[...].astype(BF16)) + b_ref[...]


def _mod(c_all, w_ada, b_ada):
    n = c_all.shape[0]
    tn = 1024
    return pl.pallas_call(
        _mod_kernel,
        grid=(6 * D // tn,),
        in_specs=[pl.BlockSpec((n, D), lambda j: (0, 0)),
                  pl.BlockSpec((D, tn), lambda j: (0, j)),
                  pl.BlockSpec((1, tn), lambda j: (0, j))],
        out_specs=pl.BlockSpec((n, tn), lambda j: (0, j)),
        out_shape=jax.ShapeDtypeStruct((n, 6 * D), F32),
        compiler_params=_params("arbitrary"),
        name="ada_mod",
    )(c_all, w_ada, b_ada)


def _mod_specs(col, tiles_per_seq):
    p = pl.BlockSpec((1, 1, D), lambda i, *_: (jnp.minimum(i // tiles_per_seq, BP - 1), 0, col))
    s = pl.BlockSpec((BS, D), lambda i, *_: (0, col))
    return p, s


def _pre_kernel(xp_ref, xs_ref, shp_ref, scp_ref, shs_ref, scs_ref, g_ref, u_ref):
    i = pl.program_id(0)
    g = g_ref[...]

    @pl.when(i < NPT)
    def _():
        u = _rms(xp_ref[...], g) * (1.0 + scp_ref[0]) + shp_ref[0]
        u_ref[...] = u.astype(BF16)

    @pl.when(i == NPT)
    def _():
        for t in range(TS):
            rows = slice(t * BS, (t + 1) * BS)
            u = _rms(xs_ref[rows, :], g) * (1.0 + scs_ref[...]) + shs_ref[...]
            u_ref[rows, :] = u.astype(BF16)


def _pre(xp2, xs2, mod_p, mod_s, g):
    shp, shs = _mod_specs(0, TPB)
    scp, scs = _mod_specs(1, TPB)
    return pl.pallas_call(
        _pre_kernel,
        grid=(NPT + 1,),
        in_specs=[pl.BlockSpec((TM, D), lambda i: (jnp.minimum(i, NPT - 1), 0)),
                  pl.BlockSpec((TM, D), lambda i: (0, 0)),
                  shp, scp, shs, scs,
                  pl.BlockSpec((1, D), lambda i: (0, 0))],
        out_specs=pl.BlockSpec((TM, D), lambda i: (i, 0)),
        out_shape=jax.ShapeDtypeStruct((R, D), BF16),
        compiler_params=_params("arbitrary"),
        name="norm1_mod",
    )(xp2, xs2, mod_p, mod_p, mod_s, mod_s, g)


def _mm_kernel(a_ref, w_ref, o_ref, wb_ref):
    @pl.when(pl.program_id(1) == 0)
    def _():
        wb_ref[...] = w_ref[...].astype(BF16)

    o_ref[...] = _dot(a_ref[...], wb_ref[...])


def _matmul(a, w, n_cols, tn, name):
    rows, k = a.shape
    return pl.pallas_call(
        _mm_kernel,
        grid=(n_cols // tn, rows // TMM),
        in_specs=[pl.BlockSpec((TMM, k), lambda j, i: (i, 0)),
                  pl.BlockSpec((k, tn), lambda j, i: (0, j))],
        out_specs=pl.BlockSpec((TMM, tn), lambda j, i: (i, j)),
        out_shape=jax.ShapeDtypeStruct((rows, n_cols), F32),
        scratch_shapes=[pltpu.VMEM((k, tn), BF16)],
        compiler_params=_params("arbitrary", "arbitrary"),
        name=name,
    )(a, w)


def _mm_nt_kernel(a_ref, w_ref, o_ref, wb_ref):
    @pl.when(pl.program_id(1) == 0)
    def _():
        wb_ref[...] = w_ref[...].astype(BF16)

    o_ref[...] = lax.dot_general(a_ref[...], wb_ref[...], NT_DIMS, preferred_element_type=F32)


def _matmul_nt(a, wt, n_cols, tn, name):
    rows, k = a.shape
    return pl.pallas_call(
        _mm_nt_kernel,
        grid=(n_cols // tn, rows // TMM),
        in_specs=[pl.BlockSpec((TMM, k), lambda j, i: (i, 0)),
                  pl.BlockSpec((tn, k), lambda j, i: (j, 0))],
        out_specs=pl.BlockSpec((TMM, tn), lambda j, i: (i, j)),
        out_shape=jax.ShapeDtypeStruct((rows, n_cols), F32),
        scratch_shapes=[pltpu.VMEM((tn, k), BF16)],
        compiler_params=_params("arbitrary", "arbitrary"),
        name=name,
    )(a, wt)


def _mm_nt_shift_kernel(shift, a_ref, w0_ref, w1_ref, o_ref, wb_ref):
    @pl.when(pl.program_id(1) == 0)
    def _():
        w = jnp.concatenate([w0_ref[shift:, :], w1_ref[...]], axis=0)
        wb_ref[...] = w.astype(BF16)

    o_ref[...] = lax.dot_general(a_ref[...], wb_ref[...], NT_DIMS, preferred_element_type=F32)


def _matmul_nt_shifted(a, wt, row0, shift, nt, rot, tn, name):
    rows, k = a.shape
    src = lambda j: lax.rem(j + rot, nt)
    return pl.pallas_call(
        functools.partial(_mm_nt_shift_kernel, shift),
        grid=(nt, rows // TMM),
        in_specs=[pl.BlockSpec((TMM, k), lambda j, i: (i, 0)),
                  pl.BlockSpec((tn, k), lambda j, i: (row0 // tn + src(j), 0)),
                  pl.BlockSpec((shift, k), lambda j, i: ((row0 + (src(j) + 1) * tn) // shift, 0))],
        out_specs=pl.BlockSpec((TMM, tn), lambda j, i: (i, j)),
        out_shape=jax.ShapeDtypeStruct((rows, nt * tn), F32),
        scratch_shapes=[pltpu.VMEM((tn, k), BF16)],
        compiler_params=_params("arbitrary", "arbitrary"),
        name=name,
    )(a, wt, wt)


def _delta_prep(get_qkv, ab, alog, dtb, valid, chunk, seq_block, put_m, l2_ref, vb_ref, kbd_ref, qd_ref, cd_ref):
    rows = ab.shape[0]
    a = ab[:, :128]
    b = ab[:, 128:]
    beta = _sigmoid(b)
    g = -jnp.exp(alog) * _softplus(a + dtb)
    if valid is not None:
        beta = jnp.where(valid, beta, 0.0)
        g = jnp.where(valid, g, 0.0)
    shift = seq_block.bit_length() - 1
    r = lax.broadcasted_iota(jnp.int32, (rows, rows), 0)
    c = lax.broadcasted_iota(jnp.int32, (rows, rows), 1)
    same = (r >> shift) == (c >> shift)
    ltri = jnp.where(same & (r >= c), 1.0, 0.0)
    lall = jnp.where(same, 1.0, 0.0)
    dcol = jnp.dot(ltri, g, preferred_element_type=F32, precision=lax.Precision.HIGHEST)
    last = jnp.dot(lall, g, preferred_element_type=F32, precision=lax.Precision.HIGHEST)
    cd_ref[...] = jnp.exp(last)
    drow = dcol.T
    edc = jnp.exp(dcol)
    etl = jnp.exp(last - dcol)
    rc = lax.broadcasted_iota(jnp.int32, (chunk, chunk), 0)
    cc = lax.broadcasted_iota(jnp.int32, (chunk, chunk), 1)
    same_c = (rc >> shift) == (cc >> shift)
    causal = same_c & (rc >= cc)
    strict = same_c & (rc > cc)
    for h in range(NH):
        cols = slice(h * DH, (h + 1) * DH)
        for sc in range(rows // chunk):
            rs = slice(sc * chunk, (sc + 1) * chunk)
            q, k, v = get_qkv(h, sc)
            qn = q * lax.rsqrt(jnp.sum(q * q, axis=-1, keepdims=True) + EPS) * (DH ** -0.5)
            kn = k * lax.rsqrt(jnp.sum(k * k, axis=-1, keepdims=True) + EPS)
            bcol = beta[rs, h:h + 1]
            ed = edc[rs, h:h + 1]
            kb = kn * bcol
            vb_ref[rs, cols] = (v * bcol).astype(BF16)
            kbd_ref[rs, cols] = (kb * ed).astype(BF16)
            qd_ref[rs, cols] = (qn * ed).astype(BF16)
            kt = kn * etl[rs, h:h + 1]
            diff = dcol[rs, h:h + 1] - drow[h:h + 1, rs]
            lm = jnp.where(causal, jnp.exp(jnp.where(causal, diff, 0.0)), 0.0)
            lhs = jnp.concatenate([kb, qn], axis=0).astype(BF16)
            s = lax.dot_general(lhs, kn.astype(BF16), NT_DIMS, preferred_element_type=F32)
            idx = sc * NH + h
            put_m(idx, jnp.where(strict, s[:chunk] * lm, 0.0))
            l2_ref[idx] = jnp.concatenate([s[chunk:] * lm, kt.T], axis=0).astype(BF16)


LANES = 128
CPB = TM // CHUNK * NH
assert 2 * CPB == LANES and TPB % 2 == 0


def _solve_lanes(mt_ref, xt_ref):
    sub = lax.broadcasted_iota(jnp.int32, (8, LANES), 0)
    zero = jnp.zeros((8, LANES), F32)
    for i in range(CHUNK):
        groups = i // 8 + 1
        acc = [jnp.where(sub + 8 * v == i, 1.0, 0.0) for v in range(groups)]
        for j in range(i):
            m = mt_ref[i, j:j + 1, :]
            for v in range(j // 8 + 1):
                acc[v] = acc[v] - m * xt_ref[j, 8 * v:8 * v + 8, :]
        for v in range(CHUNK // 8):
            xt_ref[i, 8 * v:8 * v + 8, :] = acc[v] if v < groups else zero


def _prep_prompt_kernel(cur_ref, prev_ref, ab_ref, cw_ref, alog_ref, dtb_ref,
                        t_ref, l2_ref, vb_ref, kbd_ref, qd_ref, cd_ref, xh_ref, mall_ref, mt_ref, xt_ref):
    blk = pl.program_id(1)
    odd = blk % 2
    xh_ref[0:8, :] = jnp.where(blk > 0, prev_ref[...], 0.0)
    xh_ref[8:8 + CHUNK, :] = cur_ref[0:CHUNK, :]
    w = cw_ref[...]

    def conv(sc, cols):
        src, base = (xh_ref, 8) if sc == 0 else (cur_ref, sc * CHUNK)
        y = src[base:base + CHUNK, cols] * w[CONVW - 1:CONVW, cols]
        for j in range(CONVW - 1):
            lo = base - (CONVW - 1) + j
            y = y + src[lo:lo + CHUNK, cols] * w[j:j + 1, cols]
        return _silu(y)

    def get_qkv(h, sc):
        return tuple(conv(sc, slice(p * QKW + h * DH, p * QKW + (h + 1) * DH)) for p in range(3))

    def put_m(idx, m):
        start = pl.multiple_of((odd * CPB + idx) * CHUNK, CHUNK)
        mall_ref[pl.ds(start, CHUNK), :] = m

    _delta_prep(get_qkv, ab_ref[...], alog_ref[...], dtb_ref[...], None, CHUNK, CHUNK,
                put_m, l2_ref, vb_ref, kbd_ref, qd_ref, cd_ref)

    @pl.when(odd == 1)
    def _():
        for i in range(CHUNK):
            mt_ref[i] = mall_ref[pl.ds(i, LANES, stride=CHUNK), :].T
        _solve_lanes(mt_ref, xt_ref)
        for i in range(CHUNK):
            mall_ref[pl.ds(i, LANES, stride=CHUNK), :] = xt_ref[i].T
        t_ref[...] = mall_ref[...].astype(BF16)


def _prep_prompt(proj1, proj_ab, conv_w, alog, dtb):
    nch = RP // CHUNK * NH
    row = lambda b, k: b * TPB + k
    tok = pl.BlockSpec((TM, QKW), lambda b, k: (row(b, k), 0))
    t, *rest = pl.pallas_call(
        _prep_prompt_kernel,
        grid=(BP, TPB),
        in_specs=[pl.BlockSpec((TM, CONVC), lambda b, k: (row(b, k), 0)),
                  pl.BlockSpec((8, CONVC), lambda b, k: (jnp.maximum(row(b, k) * (TM // 8) - 1, 0), 0)),
                  pl.BlockSpec((TM, 256), lambda b, k: (row(b, k), 0)),
                  pl.BlockSpec((CONVW, CONVC), lambda b, k: (0, 0)),
                  pl.BlockSpec((1, 128), lambda b, k: (0, 0)),
                  pl.BlockSpec((1, 128), lambda b, k: (0, 0))],
        out_specs=[pl.BlockSpec((LANES * CHUNK, CHUNK), lambda b, k: (row(b, k) // 2, 0)),
                   pl.BlockSpec((CPB, CHUNK + DH, CHUNK), lambda b, k: (row(b, k), 0, 0)),
                   tok, tok, tok,
                   pl.BlockSpec((TM, 128), lambda b, k: (row(b, k), 0))],
        out_shape=[jax.ShapeDtypeStruct((nch * CHUNK, CHUNK), BF16),
                   jax.ShapeDtypeStruct((nch, CHUNK + DH, CHUNK), BF16),
                   jax.ShapeDtypeStruct((RP, QKW), BF16),
                   jax.ShapeDtypeStruct((RP, QKW), BF16),
                   jax.ShapeDtypeStruct((RP, QKW), BF16),
                   jax.ShapeDtypeStruct((RP, 128), F32)],
        scratch_shapes=[pltpu.VMEM((8 + CHUNK, CONVC), F32),
                        pltpu.VMEM((LANES * CHUNK, CHUNK), F32),
                        pltpu.VMEM((CHUNK, CHUNK, LANES), F32),
                        pltpu.VMEM((CHUNK, CHUNK, LANES), F32)],
        compiler_params=_params("arbitrary", "arbitrary"),
        name="delta_prep_prompt",
    )(proj1, proj1, proj_ab, conv_w, alog, dtb)
    return (t.reshape(nch, CHUNK, CHUNK), *rest)


def _prep_sample_kernel(xp_ref, ab_ref, cw_ref, alog_ref, dtb_ref,
                        m_ref, l2_ref, vb_ref, kbd_ref, qd_ref, cd_ref):
    nrow = SSEQ * SPAD
    w = cw_ref[...]
    valid = (lax.broadcasted_iota(jnp.int32, (nrow, 1), 0) & (SPAD - 1)) < TS

    def conv(cols):
        x = xp_ref[:, cols]
        y = x * w[0:1, cols]
        for j in range(1, CONVW):
            y = y + pltpu.roll(x, nrow - j, axis=0) * w[j:j + 1, cols]
        return jnp.where(valid, _silu(y), 0.0)

    def get_qkv(h, sc):
        del sc
        return tuple(conv(slice(p * QKW + h * DH, p * QKW + (h + 1) * DH)) for p in range(3))

    def put_m(idx, m):
        m_ref[idx] = m

    _delta_prep(get_qkv, ab_ref[...], alog_ref[...], dtb_ref[...], valid, nrow, SPAD,
                put_m, l2_ref, vb_ref, kbd_ref, qd_ref, cd_ref)


def _prep_sample(xp8, ab8, conv_w, alog, dtb):
    nrow = SSEQ * SPAD
    steps = BS // SSEQ
    tok = pl.BlockSpec((nrow, QKW), lambda s: (s, 0))
    mat = pl.BlockSpec((NH, nrow, nrow), lambda s: (s, 0, 0))
    return pl.pallas_call(
        _prep_sample_kernel,
        grid=(steps,),
        in_specs=[pl.BlockSpec((nrow, CONVC), lambda s: (s, 0)),
                  pl.BlockSpec((nrow, 256), lambda s: (s, 0)),
                  pl.BlockSpec((CONVW, CONVC), lambda s: (0, 0)),
                  pl.BlockSpec((1, 128), lambda s: (0, 0)),
                  pl.BlockSpec((1, 128), lambda s: (0, 0))],
        out_specs=[mat, pl.BlockSpec((NH, nrow + DH, nrow), lambda s: (s, 0, 0)),
                   tok, tok, tok, pl.BlockSpec((nrow, 128), lambda s: (s, 0))],
        out_shape=[jax.ShapeDtypeStruct((steps * NH, nrow, nrow), F32),
                   jax.ShapeDtypeStruct((steps * NH, nrow + DH, nrow), BF16),
                   jax.ShapeDtypeStruct((BS * SPAD, QKW), BF16),
                   jax.ShapeDtypeStruct((BS * SPAD, QKW), BF16),
                   jax.ShapeDtypeStruct((BS * SPAD, QKW), BF16),
                   jax.ShapeDtypeStruct((BS * SPAD, 128), F32)],
        compiler_params=_params("arbitrary"),
        name="delta_prep_sample",
    )(xp8, ab8, conv_w, alog, dtb)


def _solve_kernel(m_ref, x_ref):
    n, _, lanes = m_ref.shape
    rowid = lax.broadcasted_iota(jnp.int32, (n, lanes), 0)

    def body_i(i, carry):
        def body_j(j, acc):
            return acc - m_ref[i, pl.ds(j, 1), :] * x_ref[j]

        x_ref[i] = lax.fori_loop(0, i, body_j, (rowid == i).astype(F32))
        return carry

    lax.fori_loop(0, n, body_i, 0)


def _solve(m, name):
    n, _, g = m.shape
    lanes = 128
    spec = pl.BlockSpec((n, n, lanes), lambda s: (0, 0, s))
    return pl.pallas_call(
        _solve_kernel,
        grid=(g // lanes,),
        in_specs=[spec],
        out_specs=spec,
        out_shape=jax.ShapeDtypeStruct((n, n, g), F32),
        compiler_params=_params("arbitrary"),
        name=name,
    )(m)


CPS = 2


def _scan_prompt_kernel(t_ref, l2_ref, vb_ref, kbd_ref, qd_ref, cd_ref, o_ref, sout_ref, s_ref, ub_ref, wd_ref):
    step = pl.program_id(0)

    @pl.when(step == 0)
    def _():
        s_ref[...] = jnp.zeros_like(s_ref)

    chains = [(b, h) for b in range(BP) for h in range(NH)]
    for c in range(CPS):
        rows = slice(c * CHUNK, (c + 1) * CHUNK)
        for n, (b, h) in enumerate(chains):
            cols = slice(h * DH, (h + 1) * DH)
            rhs = jnp.concatenate([vb_ref[b, rows, cols], kbd_ref[b, rows, cols]], axis=1)
            r1 = _dot(t_ref[b, c * NH + h], rhs)
            ub_ref[c * len(chains) + n] = r1[:, :DH]
            wd_ref[c * len(chains) + n] = r1[:, DH:].astype(BF16)
    for c in range(CPS):
        rows = slice(c * CHUNK, (c + 1) * CHUNK)
        r2s = []
        for n, (b, h) in enumerate(chains):
            cols = slice(h * DH, (h + 1) * DH)
            lhs = jnp.concatenate([wd_ref[c * len(chains) + n], qd_ref[b, rows, cols]], axis=0)
            r2s.append(_dot(lhs, s_ref[n].astype(BF16)))
        for n, (b, h) in enumerate(chains):
            cols = slice(h * DH, (h + 1) * DH)
            r2 = r2s[n]
            u = (ub_ref[c * len(chains) + n] - r2[:CHUNK]).astype(BF16)
            r3 = _dot(l2_ref[b, c * NH + h], u)
            o_ref[b, rows, cols] = r2[CHUNK:] + r3[:CHUNK]
            cd = cd_ref[b, c * CHUNK:c * CHUNK + 1, h:h + 1]
            s_ref[n] = s_ref[n] * cd + r3[CHUNK:]

    @pl.when(step == pl.num_programs(0) - 1)
    def _():
        sout_ref[...] = s_ref[...]


def _scan_prompt(t, l2, vb, kbd, qd, cd):
    nchunk = TP // CHUNK
    rows = CPS * CHUNK
    t = t.reshape(BP, nchunk * NH, CHUNK, CHUNK)
    l2 = l2.reshape(BP, nchunk * NH, CHUNK + DH, CHUNK)
    tok3 = lambda a: a.reshape(BP, TP, a.shape[-1])
    tok = pl.BlockSpec((BP, rows, QKW), lambda s: (0, s, 0))
    o, s_fin = pl.pallas_call(
        _scan_prompt_kernel,
        grid=(nchunk // CPS,),
        in_specs=[pl.BlockSpec((BP, CPS * NH, CHUNK, CHUNK), lambda s: (0, s, 0, 0)),
                  pl.BlockSpec((BP, CPS * NH, CHUNK + DH, CHUNK), lambda s: (0, s, 0, 0)),
                  tok, tok, tok,
                  pl.BlockSpec((BP, rows, 128), lambda s: (0, s, 0))],
        out_specs=[tok, pl.BlockSpec((BP * NH, DH, DH), lambda s: (0, 0, 0))],
        out_shape=[jax.ShapeDtypeStruct((BP, TP, QKW), F32),
                   jax.ShapeDtypeStruct((BP * NH, DH, DH), F32)],
        scratch_shapes=[pltpu.VMEM((BP * NH, DH, DH), F32),
                        pltpu.VMEM((CPS * BP * NH, CHUNK, DH), F32),
                        pltpu.VMEM((CPS * BP * NH, CHUNK, DH), BF16)],
        compiler_params=_params("arbitrary"),
        name="delta_scan_prompt",
    )(t, l2, tok3(vb), tok3(kbd), tok3(qd), tok3(cd))
    return o.reshape(RP, QKW), s_fin.reshape(BP, NH, DH, DH)


def _scan_sample_kernel(t_ref, l2_ref, vb_ref, kbd_ref, qd_ref, cd_ref, s0_ref, o_ref, sout_ref):
    nrow = SSEQ * SPAD
    seq_of_row = lax.broadcasted_iota(jnp.int32, (nrow, 1), 0) >> (SPAD.bit_length() - 1)
    for h in range(NH):
        cols = slice(h * DH, (h + 1) * DH)
        rhs = jnp.concatenate([vb_ref[:, cols], kbd_ref[:, cols]], axis=1)
        r1 = _dot(t_ref[h], rhs)
        qd = qd_ref[:, cols].astype(F32)
        u_parts, qs_parts = [], []
        for b in range(SSEQ):
            rows = slice(b * SPAD, (b + 1) * SPAD)
            lhs = jnp.concatenate([r1[rows, DH:], qd[rows]], axis=0).astype(BF16)
            r2 = _dot(lhs, s0_ref[b, h].astype(BF16))
            u_parts.append(r1[rows, :DH] - r2[:SPAD])
            qs_parts.append(r2[SPAD:])
        u = jnp.concatenate(u_parts, axis=0)
        o_ref[:, cols] = jnp.concatenate(qs_parts, axis=0) + _dot(l2_ref[h, :nrow, :], u.astype(BF16))
        ktt = l2_ref[h, nrow:, :]
        for b in range(SSEQ):
            ub = jnp.where(seq_of_row == b, u, 0.0).astype(BF16)
            cd = cd_ref[b * SPAD:b * SPAD + 1, h:h + 1]
            sout_ref[b, h] = s0_ref[b, h] * cd + _dot(ktt, ub)


def _scan_sample(t, l2, vb, kbd, qd, cd, s0):
    nrow = SSEQ * SPAD
    steps = BS // SSEQ
    tok = pl.BlockSpec((nrow, QKW), lambda s: (s, 0))
    st = pl.BlockSpec((SSEQ, NH, DH, DH), lambda s: (s, 0, 0, 0))
    return pl.pallas_call(
        _scan_sample_kernel,
        grid=(steps,),
        in_specs=[pl.BlockSpec((NH, nrow, nrow), lambda s: (s, 0, 0)),
                  pl.BlockSpec((NH, nrow + DH, nrow), lambda s: (s, 0, 0)),
                  tok, tok, tok, pl.BlockSpec((nrow, 128), lambda s: (s, 0)), st],
        out_specs=[tok, st],
        out_shape=[jax.ShapeDtypeStruct((BS * SPAD, QKW), F32),
                   jax.ShapeDtypeStruct((BS, NH, DH, DH), F32)],
        compiler_params=_params("arbitrary"),
        name="delta_scan_sample",
    )(t, l2, vb, kbd, qd, cd, s0)


TO = 256
NPO = RP // TO
TPO = TP // TO
POOL_PREV = 16


def _resident(shape):
    return pl.BlockSpec(shape, lambda i: (0,) * len(shape), pipeline_mode=pl.Buffered(1))


def _branch_a_act(o_ref, z_ref, og):
    parts = []
    for h in range(NH):
        cols = slice(h * DH, (h + 1) * DH)
        parts.append((_rms(o_ref[:, cols], og) * _silu(z_ref[:, cols])).astype(BF16))
    return jnp.concatenate(parts, axis=1)


def _branch_b_proj(yp, pw_ref, ps_ref, wb_ref):
    parts = [_dot(y.astype(BF16), pw_ref[gi]) for gi, y in enumerate(yp)]
    y = jnp.concatenate(parts, axis=1) * ps_ref[...]
    return _dot(y.astype(BF16), wb_ref[...])


def _branch_merge_prompt_kernel(o_ref, z_ref, x_ref, prev_ref, ga_ref, gb_ref,
                                og_ref, wa_ref, pw_ref, ps_ref, wb_ref, m_ref):
    k = pl.program_id(0) % TPO
    ya = _dot(_branch_a_act(o_ref, z_ref, og_ref[...]), wa_ref[...])

    x = x_ref[...]
    v = jnp.concatenate([jnp.where(k > 0, prev_ref[...], 0.0), x], axis=0)
    pos = k * TO + lax.broadcasted_iota(jnp.int32, (TO, 1), 0)
    yp = []
    for gi, win in enumerate(POOL_WINDOWS):
        cols = slice(gi * PGRP, (gi + 1) * PGRP)
        s = v[:, cols]
        shift = 1
        while shift < win:
            s = s + pltpu.roll(s, shift, axis=0)
            shift *= 2
        cnt = jnp.minimum(win, pos + 1).astype(F32)
        yp.append(s[POOL_PREV:] / cnt - x[:, cols])
    yb = _branch_b_proj(yp, pw_ref, ps_ref, wb_ref)

    m_ref[...] = (_sigmoid(ga_ref[...]) * ya + _sigmoid(gb_ref[...]) * yb).astype(BF16)


def _branch_merge_sample_kernel(o_ref, z_ref, x_ref, hist_ref, ga_ref, gb_ref,
                                og_ref, wa_ref, pw_ref, ps_ref, wb_ref, m_ref):
    ya =_dot(_branch_a_act(o_ref, z_ref, og_ref[...]), wa_ref[...])

    def slab(src, cols):
        if src >= PHIST:
            return x_ref[(src - PHIST) * BS:(src - PHIST + 1) * BS, cols]
        return hist_ref[src * BS:(src + 1) * BS, cols]

    yp = []
    for gi, win in enumerate(POOL_WINDOWS):
        cols = slice(gi * PGRP, (gi + 1) * PGRP)
        slabs = []
        for t in range(TS):
            acc = slab(PHIST + t, cols)
            for d in range(1, win):
                acc = acc + slab(PHIST + t - d, cols)
            slabs.append(acc / float(win) - slab(PHIST + t, cols))
        yp.append(jnp.concatenate(slabs, axis=0))
    yb = _branch_b_proj(yp, pw_ref, ps_ref, wb_ref)

    m_ref[...] = (_sigmoid(ga_ref[...]) * ya + _sigmoid(gb_ref[...]) * yb).astype(BF16)


def _branch_merge(o_p, o_s, proj1, proj3, hist_s, o_norm_g, w_proj_a, pool_w, pool_scale, w_proj_b):
    assert all(w & (w - 1) == 0 and w <= POOL_PREV for w in POOL_WINDOWS)
    xcol = 2 * D // PW
    weights = [_resident((1, DH)), _resident((QKW, D)), _resident((len(POOL_WINDOWS), PGRP, PGRP)),
               _resident((1, PW)), _resident((PW, D))]
    wargs = (o_norm_g, w_proj_a, pool_w, pool_scale, w_proj_b)
    merged_p = pl.pallas_call(
        _branch_merge_prompt_kernel,
        grid=(NPO,),
        in_specs=[pl.BlockSpec((TO, QKW), lambda i: (i, 0)),
                  pl.BlockSpec((TO, QKW), lambda i: (i, 3)),
                  pl.BlockSpec((TO, PW), lambda i: (i, xcol)),
                  pl.BlockSpec((POOL_PREV, PW), lambda i: (jnp.maximum(i * (TO // POOL_PREV) - 1, 0), xcol)),
                  pl.BlockSpec((TO, D), lambda i: (i, 0)),
                  pl.BlockSpec((TO, D), lambda i: (i, 1))] + weights,
        out_specs=pl.BlockSpec((TO, D), lambda i: (i, 0)),
        out_shape=jax.ShapeDtypeStruct((RP, D), BF16),
        compiler_params=_params("arbitrary"),
        name="branch_merge_prompt",
    )(o_p, proj1, proj3, proj3, proj3, proj3, *wargs)
    sblk = RP // RS
    merged_s = pl.pallas_call(
        _branch_merge_sample_kernel,
        grid=(1,),
        in_specs=[pl.BlockSpec((RS, QKW), lambda i: (0, 0)),
                  pl.BlockSpec((RS, QKW), lambda i: (sblk, 3)),
                  pl.BlockSpec((RS, PW), lambda i: (sblk, xcol)),
                  pl.BlockSpec((PHIST * BS, PW), lambda i: (0, 0)),
                  pl.BlockSpec((RS, D), lambda i: (sblk, 0)),
                  pl.BlockSpec((RS, D), lambda i: (sblk, 1))] + weights,
        out_specs=pl.BlockSpec((RS, D), lambda i: (0, 0)),
        out_shape=jax.ShapeDtypeStruct((RS, D), BF16),
        compiler_params=_params("arbitrary"),
        name="branch_merge_sample",
    )(o_s, proj1, proj3, hist_s, proj3, proj3, *wargs)
    return merged_p, merged_s


NCH = 512


def _proj_residual(a_ref, w_ref, x_ref, gate, out_ref):
    ss = 0.0
    for n in range(D // NCH):
        cols = slice(n * NCH, (n + 1) * NCH)
        r = x_ref[:, cols] + gate[:, cols] * _dot(a_ref[...], w_ref[:, cols])
        out_ref[:, cols] = r
        ss = ss + jnp.sum(r * r, axis=-1, keepdims=True)
    return ss


def _tile_rows(v, rows):
    return jnp.concatenate([v] * (rows // BS), axis=0)


def _post_out_kernel(mp_ref, ms_ref, xp_ref, xs_ref,
                     gtp_ref, shp_ref, scp_ref, gts_ref, shs_ref, scs_ref, g2_ref, w_ref,
                     x1_ref, h_ref):
    i = pl.program_id(0)

    def body(m_ref, x_ref, gt, sh, sc):
        ss = _proj_residual(m_ref, w_ref, x_ref, gt, x1_ref)
        inv = lax.rsqrt(ss * (1.0 / D) + EPS)
        h_ref[...] = (x1_ref[...] * inv * g2_ref[...] * (1.0 + sc) + sh).astype(BF16)

    @pl.when(i < NPT)
    def _():
        body(mp_ref, xp_ref, gtp_ref[0], shp_ref[0], scp_ref[0])

    @pl.when(i == NPT)
    def _():
        body(ms_ref, xs_ref, _tile_rows(gts_ref[...], TM), _tile_rows(shs_ref[...], TM), _tile_rows(scs_ref[...], TM))


def _post_out(merged_p, merged_s, xp2, xs2, mod_p, mod_s, norm2_g, w_out):
    assert RS == TM
    gtp, gts = _mod_specs(2, TPB)
    shp, shs = _mod_specs(3, TPB)
    scp, scs = _mod_specs(4, TPB)
    tile = pl.BlockSpec((TM, D), lambda i: (i, 0))
    ptile = pl.BlockSpec((TM, D), lambda i: (jnp.minimum(i, NPT - 1), 0))
    return pl.pallas_call(
        _post_out_kernel,
        grid=(NPT + 1,),
        in_specs=[ptile, _resident((TM, D)),
                  ptile, _resident((TM, D)),
                  gtp, shp, scp, gts, shs, scs,
                  _resident((1, D)),
                  _resident((D, D))],
        out_specs=[tile, tile],
        out_shape=[jax.ShapeDtypeStruct((R, D), F32), jax.ShapeDtypeStruct((R, D), BF16)],
        compiler_params=_params("arbitrary"),
        name="out_proj_norm2",
    )(merged_p, merged_s, xp2, xs2, mod_p, mod_p, mod_p, mod_s, mod_s, mod_s, norm2_g, w_out)


FF_TN = 512


def _ffn1_kernel(h_ref, wg_ref, wu_ref, a_ref, wgb_ref, wub_ref):
    @pl.when(pl.program_id(1) == 0)
    def _():
        wgb_ref[...] = wg_ref[...].astype(BF16)
        wub_ref[...] = wu_ref[...].astype(BF16)

    h = h_ref[...]
    a_ref[...] = (_silu(_dot(h, wgb_ref[...])) * _dot(h, wub_ref[...])).astype(BF16)


def _ffn1(h, w_gate_up):
    nj = DFF // FF_TN
    return pl.pallas_call(
        _ffn1_kernel,
        grid=(nj, R // TMM),
        in_specs=[pl.BlockSpec((TMM, D), lambda j, i: (i, 0)),
                  pl.BlockSpec((D, FF_TN), lambda j, i: (0, j)),
                  pl.BlockSpec((D, FF_TN), lambda j, i: (0, nj + j))],
        out_specs=pl.BlockSpec((TMM, FF_TN), lambda j, i: (i, j)),
        out_shape=jax.ShapeDtypeStruct((R, DFF), BF16),
        scratch_shapes=[pltpu.VMEM((D, FF_TN), BF16), pltpu.VMEM((D, FF_TN), BF16)],
        compiler_params=_params("arbitrary", "arbitrary"),
        name="ffn_gate_up",
    )(h, w_gate_up, w_gate_up)


def _ffn2_kernel(a_ref, w_ref, x1_ref, gtp_ref, gts_ref, fg_ref, yp_ref, ys_ref):
    i = pl.program_id(0)

    def body(gt, y_ref):
        ss = _proj_residual(a_ref, w_ref, x1_ref, gt, y_ref)
        y_ref[...] = y_ref[...] * lax.rsqrt(ss * (1.0 / D) + EPS) * fg_ref[...]

    @pl.when(i < NPO)
    def _():
        body(gtp_ref[0], yp_ref)

    @pl.when(i >= NPO)
    def _():
        body(_tile_rows(gts_ref[...], TO), ys_ref)


def _ffn2(act, w_down, x1, mod_p, mod_s, final_g):
    gtp, gts = _mod_specs(5, TPO)
    return pl.pallas_call(
        _ffn2_kernel,
        grid=(R // TO,),
        in_specs=[pl.BlockSpec((TO, DFF), lambda i: (i, 0)),
                  _resident((DFF, D)),
                  pl.BlockSpec((TO, D), lambda i: (i, 0)),
                  gtp, gts,
                  _resident((1, D))],
        out_specs=[pl.BlockSpec((TO, D), lambda i: (jnp.minimum(i, NPO - 1), 0)),
                   pl.BlockSpec((TO, D), lambda i: (jnp.maximum(i - NPO, 0), 0))],
        out_shape=[jax.ShapeDtypeStruct((RP, D), F32), jax.ShapeDtypeStruct((RS, D), F32)],
        compiler_params=_params("arbitrary"),
        name="ffn_down_final",
    )(act, w_down, x1, mod_p, mod_s, final_g)


def _to_time_major(x):
    return jnp.transpose(x, (1, 0, 2)).reshape(TS * BS, x.shape[-1])


def _to_batch_major(x):
    return jnp.transpose(x.reshape(TS, BS, x.shape[-1]), (1, 0, 2))


def _pad_lanes(v):
    return jnp.zeros((1, 128), F32).at[0, :NH].set(v.astype(F32))


def kernel(x_prompt, x_sample, c_prompt, c_sample, state_delta, state_conv, state_pool, w_ada, b_ada, norm1_g,
           w_in, conv_w, a_log, dt_bias, o_norm_g, pool_w, pool_scale, w_proj_a, w_proj_b, w_out, norm2_g,
           w_gate_up, w_down, final_g):
    assert w_ada.shape[0] == 1, "single layer"
    xp2 = x_prompt.reshape(RP, D)
    xs2 = _to_time_major(x_sample)

    c_all = jnp.concatenate([c_sample, c_prompt, jnp.zeros((4, D), F32)], axis=0)
    mod = _mod(c_all, w_ada[0], b_ada[0].reshape(1, 6 * D))
    mod_s = mod[:BS]
    mod_p = mod[BS:BS + BP].reshape(BP, 1, 6 * D)

    u = _pre(xp2, xs2, mod_p, mod_s, norm1_g[0].reshape(1, D))

    w_in0 = w_in[0]
    o_ab = 4 * QKW
    w_ab = jnp.zeros((D, 256), F32)
    w_ab = w_ab.at[:, :NH].set(w_in0[:, o_ab:o_ab + NH]).at[:, 128:128 + NH].set(w_in0[:, o_ab + NH:o_ab + 2 * NH])
    w_in_t = jnp.transpose(w_in0)
    proj1 = _matmul_nt(u, w_in_t, 4 * QKW, 1024, "in_proj_qkvz")
    proj_ab = _matmul(u, w_ab, 256, 256, "in_proj_ab")
    proj3 = _matmul_nt_shifted(u, w_in_t, o_ab, 2 * NH, (PW + 2 * D) // 1024, PW // 1024, 1024, "in_proj_gates_pool")

    cw = conv_w[0]
    alog = _pad_lanes(a_log[0])
    dtb = _pad_lanes(dt_bias[0])

    t_p, l2_p, vb_p, kbd_p, qd_p, cd_p = _prep_prompt(proj1, proj_ab, cw, alog, dtb)
    o_p, s_p = _scan_prompt(t_p, l2_p, vb_p, kbd_p, qd_p, cd_p)

    qkv_s = _to_batch_major(proj1[RP:, :CONVC])
    conv_in = jnp.concatenate([state_conv[0], qkv_s, jnp.zeros((BS, SPAD - TS - (CONVW - 1), CONVC), F32)], axis=1)
    ab_s = _to_batch_major(proj_ab[RP:])
    ab_s = jnp.concatenate([ab_s, jnp.zeros((BS, SPAD - TS, 256), F32)], axis=1)
    m_s, l2_s, vb_s, kbd_s, qd_s, cd_s = _prep_sample(
        conv_in.reshape(BS * SPAD, CONVC), ab_s.reshape(BS * SPAD, 256), cw, alog, dtb)
    nblk = BS // SSEQ
    m_blocks = jnp.diagonal(m_s.reshape(nblk * NH, SSEQ, SPAD, SSEQ, SPAD), axis1=1, axis2=3)
    m_blocks = jnp.transpose(m_blocks, (1, 2, 0, 3)).reshape(SPAD, SPAD, nblk * NH * SSEQ)
    t_blocks = _solve(m_blocks, "tri_solve_sample").reshape(SPAD, SPAD, nblk * NH, SSEQ)
    t_s = jnp.einsum("ijgb,bc->gbicj", t_blocks, jnp.eye(SSEQ, dtype=F32))
    t_s = t_s.reshape(nblk * NH, SSEQ * SPAD, SSEQ * SPAD).astype(BF16)
    o_s8, s_s = _scan_sample(t_s, l2_s, vb_s, kbd_s, qd_s, cd_s, state_delta[0])
    o_s = _to_time_major(o_s8.reshape(BS, SPAD, QKW)[:, :TS])

    hist_s = jnp.transpose(state_pool[0], (1, 0, 2)).reshape(PHIST * BS, PW)
    merged_p, merged_s = _branch_merge(o_p, o_s, proj1, proj3, hist_s, o_norm_g[0].reshape(1, DH),
                                       w_proj_a[0].astype(BF16), pool_w[0].astype(BF16),
                                       pool_scale[0].reshape(1, PW), w_proj_b[0].astype(BF16))
    x1, h = _post_out(merged_p, merged_s, xp2, xs2, mod_p, mod_s, norm2_g[0].reshape(1, D), w_out[0].astype(BF16))
    act = _ffn1(h, w_gate_up[0])
    y_p, y_s = _ffn2(act, w_down[0].astype(BF16), x1, mod_p, mod_s, final_g.reshape(1, D))

    xpool_s = _to_batch_major(proj3[RP:, 2 * D:])

    def tail_rows(a, n, c0, c1):
        return jnp.stack([a[(b + 1) * TP - n:(b + 1) * TP, c0:c1] for b in range(BP)])

    return (y_p.reshape(BP, TP, D),
            _to_batch_major(y_s),
            s_p[None],
            tail_rows(proj1, CONVW - 1, 0, CONVC)[None],
            tail_rows(proj3, PHIST, 2 * D, 2 * D + PW)[None],
            s_s[None],
            conv_in[:, TS:TS + CONVW - 1][None],
            jnp.concatenate([state_pool[0][:, TS:], xpool_s], axis=1)[None])
```

```python
import functools

import jax
import jax.numpy as jnp
from jax import lax
from jax.experimental import pallas as pl
from jax.experimental.pallas import tpu as pltpu

F32 = jnp.float32
BF16 = jnp.bfloat16

D = 2048
NH = 8
DH = 128
QKW = NH * DH
CONVC = 3 * QKW
CONVW = 4
PW = 1024
PGRP = 256
POOL_WINDOWS = (2, 4, 8, 16)
PHIST = 15
DFF = 5632
EPS = 1e-6

BP, TP = 4, 2048
BS, TS = 128, 4
RP = BP * TP
RS = BS * TS
R = RP + RS
TM = 512
NPT = RP // TM
TPB = TP // TM
TMM = R // 8
CHUNK = 64
SPAD = 8
SSEQ = 16

VMEM_LIMIT = 56 * 1024 * 1024

NT_DIMS = (((1,), (1,)), ((), ()))


def _params(*sem):
    return pltpu.CompilerParams(dimension_semantics=sem, vmem_limit_bytes=VMEM_LIMIT)


def _sigmoid(x):
    return 1.0 / (1.0 + jnp.exp(-x))


def _silu(x):
    return x * _sigmoid(x)


def _softplus(x):
    return jnp.maximum(x, 0.0) + jnp.log1p(jnp.exp(-jnp.abs(x)))


def _rms(x, gain):
    return x * lax.rsqrt(jnp.mean(x * x, axis=-1, keepdims=True) + EPS) * gain


def _dot(a, b):
    return jnp.dot(a, b, preferred_element_type=F32)


def _mod_kernel(c_ref, w_ref, b_ref, o_ref):
    c = c_ref[...]
    o_ref[...] = _dot(_silu(c).astype(BF16), w_ref[...].astype(BF16)) + b_ref[...]


def _mod(c_all, w_ada, b_ada):
    n = c_all.shape[0]
    tn = 1024
    return pl.pallas_call(
        _mod_kernel,
        grid=(6 * D // tn,),
        in_specs=[pl.BlockSpec((n, D), lambda j: (0, 0)),
                  pl.BlockSpec((D, tn), lambda j: (0, j)),
                  pl.BlockSpec((1, tn), lambda j: (0, j))],
        out_specs=pl.BlockSpec((n, tn), lambda j: (0, j)),
        out_shape=jax.ShapeDtypeStruct((n, 6 * D), F32),
        compiler_params=_params("arbitrary"),
        name="ada_mod",
    )(c_all, w_ada, b_ada)


def _mod_specs(col, tiles_per_seq):
    p = pl.BlockSpec((1, 1, D), lambda i, *_: (jnp.minimum(i // tiles_per_seq, BP - 1), 0, col))
    s = pl.BlockSpec((BS, D), lambda i, *_: (0, col))
    return p, s


def _pre_kernel(xp_ref, xs_ref, shp_ref, scp_ref, shs_ref, scs_ref, g_ref, u_ref):
    i = pl.program_id(0)
    g = g_ref[...]

    @pl.when(i < NPT)
    def _():
        u = _rms(xp_ref[...], g) * (1.0 + scp_ref[0]) + shp_ref[0]
        u_ref[...] = u.astype(BF16)

    @pl.when(i == NPT)
    def _():
        for t in range(TS):
            rows = slice(t * BS, (t + 1) * BS)
            u = _rms(xs_ref[rows, :], g) * (1.0 + scs_ref[...]) + shs_ref[...]
            u_ref[rows, :] = u.astype(BF16)


def _pre(xp2, xs2, mod_p, mod_s, g):
    shp, shs = _mod_specs(0, TPB)
    scp, scs = _mod_specs(1, TPB)
    return pl.pallas_call(
        _pre_kernel,
        grid=(NPT + 1,),
        in_specs=[pl.BlockSpec((TM, D), lambda i: (jnp.minimum(i, NPT - 1), 0)),
                  pl.BlockSpec((TM, D), lambda i: (0, 0)),
                  shp, scp, shs, scs,
                  pl.BlockSpec((1, D), lambda i: (0, 0))],
        out_specs=pl.BlockSpec((TM, D), lambda i: (i, 0)),
        out_shape=jax.ShapeDtypeStruct((R, D), BF16),
        compiler_params=_params("arbitrary"),
        name="norm1_mod",
    )(xp2, xs2, mod_p, mod_p, mod_s, mod_s, g)


def _mm_kernel(a_ref, w_ref, o_ref, wb_ref):
    @pl.when(pl.program_id(1) == 0)
    def _():
        wb_ref[...] = w_ref[...].astype(BF16)

    o_ref[...] = _dot(a_ref[...], wb_ref[...])


def _matmul(a, w, n_cols, tn, name):
    rows, k = a.shape
    return pl.pallas_call(
        _mm_kernel,
        grid=(n_cols // tn, rows // TMM),
        in_specs=[pl.BlockSpec((TMM, k), lambda j, i: (i, 0)),
                  pl.BlockSpec((k, tn), lambda j, i: (0, j))],
        out_specs=pl.BlockSpec((TMM, tn), lambda j, i: (i, j)),
        out_shape=jax.ShapeDtypeStruct((rows, n_cols), F32),
        scratch_shapes=[pltpu.VMEM((k, tn), BF16)],
        compiler_params=_params("arbitrary", "arbitrary"),
        name=name,
    )(a, w)


def _mm_nt_kernel(a_ref, w_ref, o_ref, wb_ref):
    @pl.when(pl.program_id(1) == 0)
    def _():
        wb_ref[...] = w_ref[...].astype(BF16)

    o_ref[...] = lax.dot_general(a_ref[...], wb_ref[...], NT_DIMS, preferred_element_type=F32)


def _matmul_nt(a, wt, n_cols, tn, name):
    rows, k = a.shape
    return pl.pallas_call(
        _mm_nt_kernel,
        grid=(n_cols // tn, rows // TMM),
        in_specs=[pl.BlockSpec((TMM, k), lambda j, i: (i, 0)),
                  pl.BlockSpec((tn, k), lambda j, i: (j, 0))],
        out_specs=pl.BlockSpec((TMM, tn), lambda j, i: (i, j)),
        out_shape=jax.ShapeDtypeStruct((rows, n_cols), F32),
        scratch_shapes=[pltpu.VMEM((tn, k), BF16)],
        compiler_params=_params("arbitrary", "arbitrary"),
        name=name,
    )(a, wt)


def _mm_nt_shift_kernel(shift, a_ref, w0_ref, w1_ref, o_ref, wb_ref):
    @pl.when(pl.program_id(1) == 0)
    def _():
        w = jnp.concatenate([w0_ref[shift:, :], w1_ref[...]], axis=0)
        wb_ref[...] = w.astype(BF16)

    o_ref[...] = lax.dot_general(a_ref[...], wb_ref[...], NT_DIMS, preferred_element_type=F32)


def _matmul_nt_shifted(a, wt, row0, shift, nt, rot, tn, name):
    rows, k = a.shape
    src = lambda j: lax.rem(j + rot, nt)
    return pl.pallas_call(
        functools.partial(_mm_nt_shift_kernel, shift),
        grid=(nt, rows // TMM),
        in_specs=[pl.BlockSpec((TMM, k), lambda j, i: (i, 0)),
                  pl.BlockSpec((tn, k), lambda j, i: (row0 // tn + src(j), 0)),
                  pl.BlockSpec((shift, k), lambda j, i: ((row0 + (src(j) + 1) * tn) // shift, 0))],
        out_specs=pl.BlockSpec((TMM, tn), lambda j, i: (i, j)),
        out_shape=jax.ShapeDtypeStruct((rows, nt * tn), F32),
        scratch_shapes=[pltpu.VMEM((tn, k), BF16)],
        compiler_params=_params("arbitrary", "arbitrary"),
        name=name,
    )(a, wt, wt)


def _delta_prep(get_qkv, ab, alog, dtb, valid, chunk, seq_block, put_m, l2_ref, vb_ref, kbd_ref, qd_ref, cd_ref):
    rows = ab.shape[0]
    a = ab[:, :128]
    b = ab[:, 128:]
    beta = _sigmoid(b)
    g = -jnp.exp(alog) * _softplus(a + dtb)
    if valid is not None:
        beta = jnp.where(valid, beta, 0.0)
        g = jnp.where(valid, g, 0.0)
    shift = seq_block.bit_length() - 1
    r = lax.broadcasted_iota(jnp.int32, (rows, rows), 0)
    c = lax.broadcasted_iota(jnp.int32, (rows, rows), 1)
    same = (r >> shift) == (c >> shift)
    ltri = jnp.where(same & (r >= c), 1.0, 0.0)
    lall = jnp.where(same, 1.0, 0.0)
    dcol = jnp.dot(ltri, g, preferred_element_type=F32, precision=lax.Precision.HIGHEST)
    last = jnp.dot(lall, g, preferred_element_type=F32, precision=lax.Precision.HIGHEST)
    cd_ref[...] = jnp.exp(last)
    drow = dcol.T
    edc = jnp.exp(dcol)
    etl = jnp.exp(last - dcol)
    rc = lax.broadcasted_iota(jnp.int32, (chunk, chunk), 0)
    cc = lax.broadcasted_iota(jnp.int32, (chunk, chunk), 1)
    same_c = (rc >> shift) == (cc >> shift)
    causal = same_c & (rc >= cc)
    strict = same_c & (rc > cc)
    for h in range(NH):
        cols = slice(h * DH, (h + 1) * DH)
        q, k, v = get_qkv(h)
        qn = q * lax.rsqrt(jnp.sum(q * q, axis=-1, keepdims=True) + EPS) * (DH ** -0.5)
        kn = k * lax.rsqrt(jnp.sum(k * k, axis=-1, keepdims=True) + EPS)
        bcol = beta[:, h:h + 1]
        kb = kn * bcol
        vb_ref[:, cols] = (v * bcol).astype(BF16)
        kbd_ref[:, cols] = (kb * edc[:, h:h + 1]).astype(BF16)
        qd_ref[:, cols] = (qn * edc[:, h:h + 1]).astype(BF16)
        kt = kn * etl[:, h:h + 1]
        for sc in range(rows // chunk):
            rs = slice(sc * chunk, (sc + 1) * chunk)
            diff = dcol[rs, h:h + 1] - drow[h:h + 1, rs]
            lm = jnp.where(causal, jnp.exp(jnp.where(causal, diff, 0.0)), 0.0)
            lhs = jnp.concatenate([kb[rs], qn[rs]], axis=0).astype(BF16)
            s = lax.dot_general(lhs, kn[rs].astype(BF16), NT_DIMS, preferred_element_type=F32)
            idx = sc * NH + h
            put_m(idx, jnp.where(strict, s[:chunk] * lm, 0.0))
            l2_ref[idx] = jnp.concatenate([s[chunk:] * lm, kt[rs].T], axis=0).astype(BF16)


LANES = 128
CPB = TM // CHUNK * NH
assert 2 * CPB == LANES and TPB % 2 == 0


def _solve_lanes(mt_ref, xt_ref):
    sub = lax.broadcasted_iota(jnp.int32, (8, LANES), 0)
    zero = jnp.zeros((8, LANES), F32)
    for i in range(CHUNK):
        groups = i // 8 + 1
        acc = [jnp.where(sub + 8 * v == i, 1.0, 0.0) for v in range(groups)]
        for j in range(i):
            m = mt_ref[i, j:j + 1, :]
            for v in range(j // 8 + 1):
                acc[v] = acc[v] - m * xt_ref[j, 8 * v:8 * v + 8, :]
        for v in range(CHUNK // 8):
            xt_ref[i, 8 * v:8 * v + 8, :] = acc[v] if v < groups else zero


def _prep_prompt_kernel(cur_ref, prev_ref, ab_ref, cw_ref, alog_ref, dtb_ref,
                        t_ref, l2_ref, vb_ref, kbd_ref, qd_ref, cd_ref, xc_ref, mall_ref, mt_ref, xt_ref):
    blk = pl.program_id(1)
    odd = blk % 2
    xc_ref[0:8, :] = jnp.where(blk > 0, prev_ref[...], 0.0)
    xc_ref[8:8 + TM, :] = cur_ref[...]
    w = cw_ref[...]

    def conv(cols):
        y = xc_ref[8:8 + TM, cols] * w[CONVW - 1:CONVW, cols]
        for j in range(CONVW - 1):
            lo = 8 - (CONVW - 1) + j
            y = y + xc_ref[lo:lo + TM, cols] * w[j:j + 1, cols]
        return _silu(y)

    def get_qkv(h):
        return tuple(conv(slice(p * QKW + h * DH, p * QKW + (h + 1) * DH)) for p in range(3))

    def put_m(idx, m):
        start = pl.multiple_of((odd * CPB + idx) * CHUNK, CHUNK)
        mall_ref[pl.ds(start, CHUNK), :] = m

    _delta_prep(get_qkv, ab_ref[...], alog_ref[...], dtb_ref[...], None, CHUNK, CHUNK,
                put_m, l2_ref, vb_ref, kbd_ref, qd_ref, cd_ref)

    @pl.when(odd == 1)
    def _():
        for i in range(CHUNK):
            mt_ref[i] = mall_ref[pl.ds(i, LANES, stride=CHUNK), :].T
        _solve_lanes(mt_ref, xt_ref)
        for i in range(CHUNK):
            mall_ref[pl.ds(i, LANES, stride=CHUNK), :] = xt_ref[i].T
        t_ref[...] = mall_ref[...].astype(BF16)


def _prep_prompt(proj1, proj_ab, conv_w, alog, dtb):
    nch = RP // CHUNK * NH
    row = lambda b, k: b * TPB + k
    tok = pl.BlockSpec((TM, QKW), lambda b, k: (row(b, k), 0))
    t, *rest = pl.pallas_call(
        _prep_prompt_kernel,
        grid=(BP, TPB),
        in_specs=[pl.BlockSpec((TM, CONVC), lambda b, k: (row(b, k), 0)),
                  pl.BlockSpec((8, CONVC), lambda b, k: (jnp.maximum(row(b, k) * (TM // 8) - 1, 0), 0)),
                  pl.BlockSpec((TM, 256), lambda b, k: (row(b, k), 0)),
                  pl.BlockSpec((CONVW, CONVC), lambda b, k: (0, 0)),
                  pl.BlockSpec((1, 128), lambda b, k: (0, 0)),
                  pl.BlockSpec((1, 128), lambda b, k: (0, 0))],
        out_specs=[pl.BlockSpec((LANES * CHUNK, CHUNK), lambda b, k: (row(b, k) // 2, 0)),
                   pl.BlockSpec((CPB, CHUNK + DH, CHUNK), lambda b, k: (row(b, k), 0, 0)),
                   tok, tok, tok,
                   pl.BlockSpec((TM, 128), lambda b, k: (row(b, k), 0))],
        out_shape=[jax.ShapeDtypeStruct((nch * CHUNK, CHUNK), BF16),
                   jax.ShapeDtypeStruct((nch, CHUNK + DH, CHUNK), BF16),
                   jax.ShapeDtypeStruct((RP, QKW), BF16),
                   jax.ShapeDtypeStruct((RP, QKW), BF16),
                   jax.ShapeDtypeStruct((RP, QKW), BF16),
                   jax.ShapeDtypeStruct((RP, 128), F32)],
        scratch_shapes=[pltpu.VMEM((8 + TM, CONVC), F32),
                        pltpu.VMEM((LANES * CHUNK, CHUNK), F32),
                        pltpu.VMEM((CHUNK, CHUNK, LANES), F32),
                        pltpu.VMEM((CHUNK, CHUNK, LANES), F32)],
        compiler_params=_params("arbitrary", "arbitrary"),
        name="delta_prep_prompt",
    )(proj1, proj1, proj_ab, conv_w, alog, dtb)
    return (t.reshape(nch, CHUNK, CHUNK), *rest)


def _prep_sample_kernel(xp_ref, ab_ref, cw_ref, alog_ref, dtb_ref,
                        m_ref, l2_ref, vb_ref, kbd_ref, qd_ref, cd_ref):
    nrow = SSEQ * SPAD
    w = cw_ref[...]
    valid = (lax.broadcasted_iota(jnp.int32, (nrow, 1), 0) & (SPAD - 1)) < TS

    def conv(cols):
        x = xp_ref[:, cols]
        y = x * w[0:1, cols]
        for j in range(1, CONVW):
            y = y + pltpu.roll(x, nrow - j, axis=0) * w[j:j + 1, cols]
        return jnp.where(valid, _silu(y), 0.0)

    def get_qkv(h):
        return tuple(conv(slice(p * QKW + h * DH, p * QKW + (h + 1) * DH)) for p in range(3))

    def put_m(idx, m):
        m_ref[idx] = m

    _delta_prep(get_qkv, ab_ref[...], alog_ref[...], dtb_ref[...], valid, nrow, SPAD,
                put_m, l2_ref, vb_ref, kbd_ref, qd_ref, cd_ref)


def _prep_sample(xp8, ab8, conv_w, alog, dtb):
    nrow = SSEQ * SPAD
    steps = BS // SSEQ
    tok = pl.BlockSpec((nrow, QKW), lambda s: (s, 0))
    mat = pl.BlockSpec((NH, nrow, nrow), lambda s: (s, 0, 0))
    return pl.pallas_call(
        _prep_sample_kernel,
        grid=(steps,),
        in_specs=[pl.BlockSpec((nrow, CONVC), lambda s: (s, 0)),
                  pl.BlockSpec((nrow, 256), lambda s: (s, 0)),
                  pl.BlockSpec((CONVW, CONVC), lambda s: (0, 0)),
                  pl.BlockSpec((1, 128), lambda s: (0, 0)),
                  pl.BlockSpec((1, 128), lambda s: (0, 0))],
        out_specs=[mat, pl.BlockSpec((NH, nrow + DH, nrow), lambda s: (s, 0, 0)),
                   tok, tok, tok, pl.BlockSpec((nrow, 128), lambda s: (s, 0))],
        out_shape=[jax.ShapeDtypeStruct((steps * NH, nrow, nrow), F32),
                   jax.ShapeDtypeStruct((steps * NH, nrow + DH, nrow), BF16),
                   jax.ShapeDtypeStruct((BS * SPAD, QKW), BF16),
                   jax.ShapeDtypeStruct((BS * SPAD, QKW), BF16),
                   jax.ShapeDtypeStruct((BS * SPAD, QKW), BF16),
                   jax.ShapeDtypeStruct((BS * SPAD, 128), F32)],
        compiler_params=_params("arbitrary"),
        name="delta_prep_sample",
    )(xp8, ab8, conv_w, alog, dtb)


def _solve_kernel(m_ref, x_ref):
    n, _, lanes = m_ref.shape
    rowid = lax.broadcasted_iota(jnp.int32, (n, lanes), 0)

    def body_i(i, carry):
        def body_j(j, acc):
            return acc - m_ref[i, pl.ds(j, 1), :] * x_ref[j]

        x_ref[i] = lax.fori_loop(0, i, body_j, (rowid == i).astype(F32))
        return carry

    lax.fori_loop(0, n, body_i, 0)


def _solve(m, name):
    n, _, g = m.shape
    lanes = 128
    spec = pl.BlockSpec((n, n, lanes), lambda s: (0, 0, s))
    return pl.pallas_call(
        _solve_kernel,
        grid=(g // lanes,),
        in_specs=[spec],
        out_specs=spec,
        out_shape=jax.ShapeDtypeStruct((n, n, g), F32),
        compiler_params=_params("arbitrary"),
        name=name,
    )(m)


CPS = 2


def _scan_prompt_kernel(t_ref, l2_ref, vb_ref, kbd_ref, qd_ref, cd_ref, o_ref, sout_ref, s_ref, ub_ref, wd_ref):
    step = pl.program_id(0)

    @pl.when(step == 0)
    def _():
        s_ref[...] = jnp.zeros_like(s_ref)

    chains = [(b, h) for b in range(BP) for h in range(NH)]
    for c in range(CPS):
        rows = slice(c * CHUNK, (c + 1) * CHUNK)
        for n, (b, h) in enumerate(chains):
            cols = slice(h * DH, (h + 1) * DH)
            rhs = jnp.concatenate([vb_ref[b, rows, cols], kbd_ref[b, rows, cols]], axis=1)
            r1 = _dot(t_ref[b, c * NH + h], rhs)
            ub_ref[c * len(chains) + n] = r1[:, :DH]
            wd_ref[c * len(chains) + n] = r1[:, DH:].astype(BF16)
    for c in range(CPS):
        rows = slice(c * CHUNK, (c + 1) * CHUNK)
        r2s = []
        for n, (b, h) in enumerate(chains):
            cols = slice(h * DH, (h + 1) * DH)
            lhs = jnp.concatenate([wd_ref[c * len(chains) + n], qd_ref[b, rows, cols]], axis=0)
            r2s.append(_dot(lhs, s_ref[n].astype(BF16)))
        for n, (b, h) in enumerate(chains):
            cols = slice(h * DH, (h + 1) * DH)
            r2 = r2s[n]
            u = (ub_ref[c * len(chains) + n] - r2[:CHUNK]).astype(BF16)
            r3 = _dot(l2_ref[b, c * NH + h], u)
            o_ref[b, rows, cols] = r2[CHUNK:] + r3[:CHUNK]
            cd = cd_ref[b, c * CHUNK:c * CHUNK + 1, h:h + 1]
            s_ref[n] = s_ref[n] * cd + r3[CHUNK:]

    @pl.when(step == pl.num_programs(0) - 1)
    def _():
        sout_ref[...] = s_ref[...]


def _scan_prompt(t, l2, vb, kbd, qd, cd):
    nchunk = TP // CHUNK
    rows = CPS * CHUNK
    t = t.reshape(BP, nchunk * NH, CHUNK, CHUNK)
    l2 = l2.reshape(BP, nchunk * NH, CHUNK + DH, CHUNK)
    tok3 = lambda a: a.reshape(BP, TP, a.shape[-1])
    tok = pl.BlockSpec((BP, rows, QKW), lambda s: (0, s, 0))
    o, s_fin = pl.pallas_call(
        _scan_prompt_kernel,
        grid=(nchunk // CPS,),
        in_specs=[pl.BlockSpec((BP, CPS * NH, CHUNK, CHUNK), lambda s: (0, s, 0, 0)),
                  pl.BlockSpec((BP, CPS * NH, CHUNK + DH, CHUNK), lambda s: (0, s, 0, 0)),
                  tok, tok, tok,
                  pl.BlockSpec((BP, rows, 128), lambda s: (0, s, 0))],
        out_specs=[tok, pl.BlockSpec((BP * NH, DH, DH), lambda s: (0, 0, 0))],
        out_shape=[jax.ShapeDtypeStruct((BP, TP, QKW), F32),
                   jax.ShapeDtypeStruct((BP * NH, DH, DH), F32)],
        scratch_shapes=[pltpu.VMEM((BP * NH, DH, DH), F32),
                        pltpu.VMEM((CPS * BP * NH, CHUNK, DH), F32),
                        pltpu.VMEM((CPS * BP * NH, CHUNK, DH), BF16)],
        compiler_params=_params("arbitrary"),
        name="delta_scan_prompt",
    )(t, l2, tok3(vb), tok3(kbd), tok3(qd), tok3(cd))
    return o.reshape(RP, QKW), s_fin.reshape(BP, NH, DH, DH)


def _scan_sample_kernel(t_ref, l2_ref, vb_ref, kbd_ref, qd_ref, cd_ref, s0_ref, o_ref, sout_ref):
    nrow = SSEQ * SPAD
    seq_of_row = lax.broadcasted_iota(jnp.int32, (nrow, 1), 0) >> (SPAD.bit_length() - 1)
    for h in range(NH):
        cols = slice(h * DH, (h + 1) * DH)
        rhs = jnp.concatenate([vb_ref[:, cols], kbd_ref[:, cols]], axis=1)
        r1 = _dot(t_ref[h], rhs)
        qd = qd_ref[:, cols].astype(F32)
        u_parts, qs_parts = [], []
        for b in range(SSEQ):
            rows = slice(b * SPAD, (b + 1) * SPAD)
            lhs = jnp.concatenate([r1[rows, DH:], qd[rows]], axis=0).astype(BF16)
            r2 = _dot(lhs, s0_ref[b, h].astype(BF16))
            u_parts.append(r1[rows, :DH] - r2[:SPAD])
            qs_parts.append(r2[SPAD:])
        u = jnp.concatenate(u_parts, axis=0)
        o_ref[:, cols] = jnp.concatenate(qs_parts, axis=0) + _dot(l2_ref[h, :nrow, :], u.astype(BF16))
        ktt = l2_ref[h, nrow:, :]
        for b in range(SSEQ):
            ub = jnp.where(seq_of_row == b, u, 0.0).astype(BF16)
            cd = cd_ref[b * SPAD:b * SPAD + 1, h:h + 1]
            sout_ref[b, h] = s0_ref[b, h] * cd + _dot(ktt, ub)


def _scan_sample(t, l2, vb, kbd, qd, cd, s0):
    nrow = SSEQ * SPAD
    steps = BS // SSEQ
    tok = pl.BlockSpec((nrow, QKW), lambda s: (s, 0))
    st = pl.BlockSpec((SSEQ, NH, DH, DH), lambda s: (s, 0, 0, 0))
    return pl.pallas_call(
        _scan_sample_kernel,
        grid=(steps,),
        in_specs=[pl.BlockSpec((NH, nrow, nrow), lambda s: (s, 0, 0)),
                  pl.BlockSpec((NH, nrow + DH, nrow), lambda s: (s, 0, 0)),
                  tok, tok, tok, pl.BlockSpec((nrow, 128), lambda s: (s, 0)), st],
        out_specs=[tok, st],
        out_shape=[jax.ShapeDtypeStruct((BS * SPAD, QKW), F32),
                   jax.ShapeDtypeStruct((BS, NH, DH, DH), F32)],
        compiler_params=_params("arbitrary"),
        name="delta_scan_sample",
    )(t, l2, vb, kbd, qd, cd, s0)


TO = 256
NPO = RP // TO
TPO = TP // TO
POOL_PREV = 16


def _resident(shape):
    return pl.BlockSpec(shape, lambda i: (0,) * len(shape), pipeline_mode=pl.Buffered(1))


def _branch_a_act(o_ref, z_ref, og):
    parts = []
    for h in range(NH):
        cols = slice(h * DH, (h + 1) * DH)
        parts.append((_rms(o_ref[:, cols], og) * _silu(z_ref[:, cols])).astype(BF16))
    return jnp.concatenate(parts, axis=1)


def _branch_b_proj(yp, pw_ref, ps_ref, wb_ref):
    parts = [_dot(y.astype(BF16), pw_ref[gi]) for gi, y in enumerate(yp)]
    y = jnp.concatenate(parts, axis=1) * ps_ref[...]
    return _dot(y.astype(BF16), wb_ref[...])


def _branch_merge_prompt_kernel(o_ref, z_ref, x_ref, prev_ref, ga_ref, gb_ref,
                                og_ref, wa_ref, pw_ref, ps_ref, wb_ref, m_ref):
    k = pl.program_id(0) % TPO
    ya = _dot(_branch_a_act(o_ref, z_ref, og_ref[...]), wa_ref[...])

    x = x_ref[...]
    v = jnp.concatenate([jnp.where(k > 0, prev_ref[...], 0.0), x], axis=0)
    pos = k * TO + lax.broadcasted_iota(jnp.int32, (TO, 1), 0)
    yp = []
    for gi, win in enumerate(POOL_WINDOWS):
        cols = slice(gi * PGRP, (gi + 1) * PGRP)
        s = v[:, cols]
        shift = 1
        while shift < win:
            s = s + pltpu.roll(s, shift, axis=0)
            shift *= 2
        cnt = jnp.minimum(win, pos + 1).astype(F32)
        yp.append(s[POOL_PREV:] / cnt - x[:, cols])
    yb = _branch_b_proj(yp, pw_ref, ps_ref, wb_ref)

    m_ref[...] = (_sigmoid(ga_ref[...]) * ya + _sigmoid(gb_ref[...]) * yb).astype(BF16)


def _branch_merge_sample_kernel(o_ref, z_ref, x_ref, hist_ref, ga_ref, gb_ref,
                                og_ref, wa_ref, pw_ref, ps_ref, wb_ref, m_ref):
    ya =_dot(_branch_a_act(o_ref, z_ref, og_ref[...]), wa_ref[...])

    def slab(src, cols):
        if src >= PHIST:
            return x_ref[(src - PHIST) * BS:(src - PHIST + 1) * BS, cols]
        return hist_ref[src * BS:(src + 1) * BS, cols]

    yp = []
    for gi, win in enumerate(POOL_WINDOWS):
        cols = slice(gi * PGRP, (gi + 1) * PGRP)
        slabs = []
        for t in range(TS):
            acc = slab(PHIST + t, cols)
            for d in range(1, win):
                acc = acc + slab(PHIST + t - d, cols)
            slabs.append(acc / float(win) - slab(PHIST + t, cols))
        yp.append(jnp.concatenate(slabs, axis=0))
    yb = _branch_b_proj(yp, pw_ref, ps_ref, wb_ref)

    m_ref[...] = (_sigmoid(ga_ref[...]) * ya + _sigmoid(gb_ref[...]) * yb).astype(BF16)


def _branch_merge(o_p, o_s, proj1, proj3, hist_s, o_norm_g, w_proj_a, pool_w, pool_scale, w_proj_b):
    assert all(w & (w - 1) == 0 and w <= POOL_PREV for w in POOL_WINDOWS)
    xcol = 2 * D // PW
    weights = [_resident((1, DH)), _resident((QKW, D)), _resident((len(POOL_WINDOWS), PGRP, PGRP)),
               _resident((1, PW)), _resident((PW, D))]
    wargs = (o_norm_g, w_proj_a, pool_w, pool_scale, w_proj_b)
    merged_p = pl.pallas_call(
        _branch_merge_prompt_kernel,
        grid=(NPO,),
        in_specs=[pl.BlockSpec((TO, QKW), lambda i: (i, 0)),
                  pl.BlockSpec((TO, QKW), lambda i: (i, 3)),
                  pl.BlockSpec((TO, PW), lambda i: (i, xcol)),
                  pl.BlockSpec((POOL_PREV, PW), lambda i: (jnp.maximum(i * (TO // POOL_PREV) - 1, 0), xcol)),
                  pl.BlockSpec((TO, D), lambda i: (i, 0)),
                  pl.BlockSpec((TO, D), lambda i: (i, 1))] + weights,
        out_specs=pl.BlockSpec((TO, D), lambda i: (i, 0)),
        out_shape=jax.ShapeDtypeStruct((RP, D), BF16),
        compiler_params=_params("arbitrary"),
        name="branch_merge_prompt",
    )(o_p, proj1, proj3, proj3, proj3, proj3, *wargs)
    sblk = RP // RS
    merged_s = pl.pallas_call(
        _branch_merge_sample_kernel,
        grid=(1,),
        in_specs=[pl.BlockSpec((RS, QKW), lambda i: (0, 0)),
                  pl.BlockSpec((RS, QKW), lambda i: (sblk, 3)),
                  pl.BlockSpec((RS, PW), lambda i: (sblk, xcol)),
                  pl.BlockSpec((PHIST * BS, PW), lambda i: (0, 0)),
                  pl.BlockSpec((RS, D), lambda i: (sblk, 0)),
                  pl.BlockSpec((RS, D), lambda i: (sblk, 1))] + weights,
        out_specs=pl.BlockSpec((RS, D), lambda i: (0, 0)),
        out_shape=jax.ShapeDtypeStruct((RS, D), BF16),
        compiler_params=_params("arbitrary"),
        name="branch_merge_sample",
    )(o_s, proj1, proj3, hist_s, proj3, proj3, *wargs)
    return merged_p, merged_s


NCH = 512


def _proj_residual(a_ref, w_ref, x_ref, gate, out_ref):
    ss = 0.0
    for n in range(D // NCH):
        cols = slice(n * NCH, (n + 1) * NCH)
        r = x_ref[:, cols] + gate[:, cols] * _dot(a_ref[...], w_ref[:, cols])
        out_ref[:, cols] = r
        ss = ss + jnp.sum(r * r, axis=-1, keepdims=True)
    return ss


def _tile_rows(v, rows):
    return jnp.concatenate([v] * (rows // BS), axis=0)


def _post_out_kernel(mp_ref, ms_ref, xp_ref, xs_ref,
                     gtp_ref, shp_ref, scp_ref, gts_ref, shs_ref, scs_ref, g2_ref, w_ref,
                     x1_ref, h_ref):
    i = pl.program_id(0)

    def body(m_ref, x_ref, gt, sh, sc):
        ss = _proj_residual(m_ref, w_ref, x_ref, gt, x1_ref)
        inv = lax.rsqrt(ss * (1.0 / D) + EPS)
        h_ref[...] = (x1_ref[...] * inv * g2_ref[...] * (1.0 + sc) + sh).astype(BF16)

    @pl.when(i < NPT)
    def _():
        body(mp_ref, xp_ref, gtp_ref[0], shp_ref[0], scp_ref[0])

    @pl.when(i == NPT)
    def _():
        body(ms_ref, xs_ref, _tile_rows(gts_ref[...], TM), _tile_rows(shs_ref[...], TM), _tile_rows(scs_ref[...], TM))


def _post_out(merged_p, merged_s, xp2, xs2, mod_p, mod_s, norm2_g, w_out):
    assert RS == TM
    gtp, gts = _mod_specs(2, TPB)
    shp, shs = _mod_specs(3, TPB)
    scp, scs = _mod_specs(4, TPB)
    tile = pl.BlockSpec((TM, D), lambda i: (i, 0))
    ptile = pl.BlockSpec((TM, D), lambda i: (jnp.minimum(i, NPT - 1), 0))
    return pl.pallas_call(
        _post_out_kernel,
        grid=(NPT + 1,),
        in_specs=[ptile, _resident((TM, D)),
                  ptile, _resident((TM, D)),
                  gtp, shp, scp, gts, shs, scs,
                  _resident((1, D)),
                  _resident((D, D))],
        out_specs=[tile, tile],
        out_shape=[jax.ShapeDtypeStruct((R, D), F32), jax.ShapeDtypeStruct((R, D), BF16)],
        compiler_params=_params("arbitrary"),
        name="out_proj_norm2",
    )(merged_p, merged_s, xp2, xs2, mod_p, mod_p, mod_p, mod_s, mod_s, mod_s, norm2_g, w_out)


FF_TN = 512


def _ffn1_kernel(h_ref, wg_ref, wu_ref, a_ref, wgb_ref, wub_ref):
    @pl.when(pl.program_id(1) == 0)
    def _():
        wgb_ref[...] = wg_ref[...].astype(BF16)
        wub_ref[...] = wu_ref[...].astype(BF16)

    h = h_ref[...]
    a_ref[...] = (_silu(_dot(h, wgb_ref[...])) * _dot(h, wub_ref[...])).astype(BF16)


def _ffn1(h, w_gate_up):
    nj = DFF // FF_TN
    return pl.pallas_call(
        _ffn1_kernel,
        grid=(nj, R // TMM),
        in_specs=[pl.BlockSpec((TMM, D), lambda j, i: (i, 0)),
                  pl.BlockSpec((D, FF_TN), lambda j, i: (0, j)),
                  pl.BlockSpec((D, FF_TN), lambda j, i: (0, nj + j))],
        out_specs=pl.BlockSpec((TMM, FF_TN), lambda j, i: (i, j)),
        out_shape=jax.ShapeDtypeStruct((R, DFF), BF16),
        scratch_shapes=[pltpu.VMEM((D, FF_TN), BF16), pltpu.VMEM((D, FF_TN), BF16)],
        compiler_params=_params("arbitrary", "arbitrary"),
        name="ffn_gate_up",
    )(h, w_gate_up, w_gate_up)


def _ffn2_kernel(a_ref, w_ref, x1_ref, gtp_ref, gts_ref, fg_ref, yp_ref, ys_ref):
    i = pl.program_id(0)

    def body(gt, y_ref):
        ss = _proj_residual(a_ref, w_ref, x1_ref, gt, y_ref)
        y_ref[...] = y_ref[...] * lax.rsqrt(ss * (1.0 / D) + EPS) * fg_ref[...]

    @pl.when(i < NPO)
    def _():
        body(gtp_ref[0], yp_ref)

    @pl.when(i >= NPO)
    def _():
        body(_tile_rows(gts_ref[...], TO), ys_ref)


def _ffn2(act, w_down, x1, mod_p, mod_s, final_g):
    gtp, gts = _mod_specs(5, TPO)
    return pl.pallas_call(
        _ffn2_kernel,
        grid=(R // TO,),
        in_specs=[pl.BlockSpec((TO, DFF), lambda i: (i, 0)),
                  _resident((DFF, D)),
                  pl.BlockSpec((TO, D), lambda i: (i, 0)),
                  gtp, gts,
                  _resident((1, D))],
        out_specs=[pl.BlockSpec((TO, D), lambda i: (jnp.minimum(i, NPO - 1), 0)),
                   pl.BlockSpec((TO, D), lambda i: (jnp.maximum(i - NPO, 0), 0))],
        out_shape=[jax.ShapeDtypeStruct((RP, D), F32), jax.ShapeDtypeStruct((RS, D), F32)],
        compiler_params=_params("arbitrary"),
        name="ffn_down_final",
    )(act, w_down, x1, mod_p, mod_s, final_g)


def _to_time_major(x):
    return jnp.transpose(x, (1, 0, 2)).reshape(TS * BS, x.shape[-1])


def _to_batch_major(x):
    return jnp.transpose(x.reshape(TS, BS, x.shape[-1]), (1, 0, 2))


def _pad_lanes(v):
    return jnp.zeros((1, 128), F32).at[0, :NH].set(v.astype(F32))


def kernel(x_prompt, x_sample, c_prompt, c_sample, state_delta, state_conv, state_pool, w_ada, b_ada, norm1_g,
           w_in, conv_w, a_log, dt_bias, o_norm_g, pool_w, pool_scale, w_proj_a, w_proj_b, w_out, norm2_g,
           w_gate_up, w_down, final_g):
    assert w_ada.shape[0] == 1, "single layer"
    xp2 = x_prompt.reshape(RP, D)
    xs2 = _to_time_major(x_sample)

    c_all = jnp.concatenate([c_sample, c_prompt, jnp.zeros((4, D), F32)], axis=0)
    mod = _mod(c_all, w_ada[0], b_ada[0].reshape(1, 6 * D))
    mod_s = mod[:BS]
    mod_p = mod[BS:BS + BP].reshape(BP, 1, 6 * D)

    u = _pre(xp2, xs2, mod_p, mod_s, norm1_g[0].reshape(1, D))

    w_in0 = w_in[0]
    o_ab = 4 * QKW
    w_ab = jnp.zeros((D, 256), F32)
    w_ab = w_ab.at[:, :NH].set(w_in0[:, o_ab:o_ab + NH]).at[:, 128:128 + NH].set(w_in0[:, o_ab + NH:o_ab + 2 * NH])
    w_in_t = jnp.transpose(w_in0)
    proj1 = _matmul_nt(u, w_in_t, 4 * QKW, 1024, "in_proj_qkvz")
    proj_ab = _matmul(u, w_ab, 256, 256, "in_proj_ab")
    proj3 = _matmul_nt_shifted(u, w_in_t, o_ab, 2 * NH, (PW + 2 * D) // 1024, PW // 1024, 1024, "in_proj_gates_pool")

    cw = conv_w[0]
    alog = _pad_lanes(a_log[0])
    dtb = _pad_lanes(dt_bias[0])

    t_p, l2_p, vb_p, kbd_p, qd_p, cd_p = _prep_prompt(proj1, proj_ab, cw, alog, dtb)
    o_p, s_p = _scan_prompt(t_p, l2_p, vb_p, kbd_p, qd_p, cd_p)

    qkv_s = _to_batch_major(proj1[RP:, :CONVC])
    conv_in = jnp.concatenate([state_conv[0], qkv_s, jnp.zeros((BS, SPAD - TS - (CONVW - 1), CONVC), F32)], axis=1)
    ab_s = _to_batch_major(proj_ab[RP:])
    ab_s = jnp.concatenate([ab_s, jnp.zeros((BS, SPAD - TS, 256), F32)], axis=1)
    m_s, l2_s, vb_s, kbd_s, qd_s, cd_s = _prep_sample(
        conv_in.reshape(BS * SPAD, CONVC), ab_s.reshape(BS * SPAD, 256), cw, alog, dtb)
    nblk = BS // SSEQ
    m_blocks = jnp.diagonal(m_s.reshape(nblk * NH, SSEQ, SPAD, SSEQ, SPAD), axis1=1, axis2=3)
    m_blocks = jnp.transpose(m_blocks, (1, 2, 0, 3)).reshape(SPAD, SPAD, nblk * NH * SSEQ)
    t_blocks = _solve(m_blocks, "tri_solve_sample").reshape(SPAD, SPAD, nblk * NH, SSEQ)
    t_s = jnp.einsum("ijgb,bc->gbicj", t_blocks, jnp.eye(SSEQ, dtype=F32))
    t_s = t_s.reshape(nblk * NH, SSEQ * SPAD, SSEQ * SPAD).astype(BF16)
    o_s8, s_s = _scan_sample(t_s, l2_s, vb_s, kbd_s, qd_s, cd_s, state_delta[0])
    o_s = _to_time_major(o_s8.reshape(BS, SPAD, QKW)[:, :TS])

    hist_s = jnp.transpose(state_pool[0], (1, 0, 2)).reshape(PHIST * BS, PW)
    merged_p, merged_s = _branch_merge(o_p, o_s, proj1, proj3, hist_s, o_norm_g[0].reshape(1, DH),
                                       w_proj_a[0].astype(BF16), pool_w[0].astype(BF16),
                                       pool_scale[0].reshape(1, PW), w_proj_b[0].astype(BF16))
    x1, h = _post_out(merged_p, merged_s, xp2, xs2, mod_p, mod_s, norm2_g[0].reshape(1, D), w_out[0].astype(BF16))
    act = _ffn1(h, w_gate_up[0])
    y_p, y_s = _ffn2(act, w_down[0].astype(BF16), x1, mod_p, mod_s, final_g.reshape(1, D))

    xpool_s = _to_batch_major(proj3[RP:, 2 * D:])

    def tail_rows(a, n, c0, c1):
        return jnp.stack([a[(b + 1) * TP - n:(b + 1) * TP, c0:c1] for b in range(BP)])

    return (y_p.reshape(BP, TP, D),
            _to_batch_major(y_s),
            s_p[None],
            tail_rows(proj1, CONVW - 1, 0, CONVC)[None],
            tail_rows(proj3, PHIST, 2 * D, 2 * D + PW)[None],
            s_s[None],
            conv_in[:, TS:TS + CONVW - 1][None],
            jnp.concatenate([state_pool[0][:, TS:], xpool_s], axis=1)[None])
```

```python
import functools

import jax
import jax.numpy as jnp
from jax import lax
from jax.experimental import pallas as pl
from jax.experimental.pallas import tpu as pltpu

F32 = jnp.float32
BF16 = jnp.bfloat16

D = 2048
NH = 8
DH = 128
QKW = NH * DH
CONVC = 3 * QKW
CONVW = 4
PW = 1024
PGRP = 256
POOL_WINDOWS = (2, 4, 8, 16)
PHIST = 15
DFF = 5632
EPS = 1e-6

BP, TP = 4, 2048
BS, TS = 128, 4
RP = BP * TP
RS = BS * TS
R = RP + RS
TM = 512
NPT = RP // TM
TPB = TP // TM
TMM = R // 8
CHUNK = 64
SPAD = 8
SSEQ = 16

VMEM_LIMIT = 56 * 1024 * 1024

NT_DIMS = (((1,), (1,)), ((), ()))


def _params(*sem):
    return pltpu.CompilerParams(dimension_semantics=sem, vmem_limit_bytes=VMEM_LIMIT)


def _sigmoid(x):
    return 1.0 / (1.0 + jnp.exp(-x))


def _silu(x):
    return x * _sigmoid(x)


def _softplus(x):
    return jnp.maximum(x, 0.0) + jnp.log1p(jnp.exp(-jnp.abs(x)))


def _rms(x, gain):
    return x * lax.rsqrt(jnp.mean(x * x, axis=-1, keepdims=True) + EPS) * gain


def _dot(a, b):
    return jnp.dot(a, b, preferred_element_type=F32)


def _mod_kernel(c_ref, w_ref, b_ref, o_ref):
    c = c_ref[...]
    o_ref[...] = _dot(_silu(c).astype(BF16), w_ref[...].astype(BF16)) + b_ref[...]


def _mod(c_all, w_ada, b_ada):
    n = c_all.shape[0]
    tn = 1024
    return pl.pallas_call(
        _mod_kernel,
        grid=(6 * D // tn,),
        in_specs=[pl.BlockSpec((n, D), lambda j: (0, 0)),
                  pl.BlockSpec((D, tn), lambda j: (0, j)),
                  pl.BlockSpec((1, tn), lambda j: (0, j))],
        out_specs=pl.BlockSpec((n, tn), lambda j: (0, j)),
        out_shape=jax.ShapeDtypeStruct((n, 6 * D), F32),
        compiler_params=_params("arbitrary"),
        name="ada_mod",
    )(c_all, w_ada, b_ada)


def _mod_specs(col, tiles_per_seq):
    p = pl.BlockSpec((1, 1, D), lambda i, *_: (jnp.minimum(i // tiles_per_seq, BP - 1), 0, col))
    s = pl.BlockSpec((BS, D), lambda i, *_: (0, col))
    return p, s


def _pre_kernel(xp_ref, xs_ref, shp_ref, scp_ref, shs_ref, scs_ref, g_ref, u_ref):
    i = pl.program_id(0)
    g = g_ref[...]

    @pl.when(i < NPT)
    def _():
        u = _rms(xp_ref[...], g) * (1.0 + scp_ref[0]) + shp_ref[0]
        u_ref[...] = u.astype(BF16)

    @pl.when(i == NPT)
    def _():
        for t in range(TS):
            rows = slice(t * BS, (t + 1) * BS)
            u = _rms(xs_ref[rows, :], g) * (1.0 + scs_ref[...]) + shs_ref[...]
            u_ref[rows, :] = u.astype(BF16)


def _pre(xp2, xs2, mod_p, mod_s, g):
    shp, shs = _mod_specs(0, TPB)
    scp, scs = _mod_specs(1, TPB)
    return pl.pallas_call(
        _pre_kernel,
        grid=(NPT + 1,),
        in_specs=[pl.BlockSpec((TM, D), lambda i: (jnp.minimum(i, NPT - 1), 0)),
                  pl.BlockSpec((TM, D), lambda i: (0, 0)),
                  shp, scp, shs, scs,
                  pl.BlockSpec((1, D), lambda i: (0, 0))],
        out_specs=pl.BlockSpec((TM, D), lambda i: (i, 0)),
        out_shape=jax.ShapeDtypeStruct((R, D), BF16),
        compiler_params=_params("arbitrary"),
        name="norm1_mod",
    )(xp2, xs2, mod_p, mod_p, mod_s, mod_s, g)


def _mm_kernel(a_ref, w_ref, o_ref, wb_ref):
    @pl.when(pl.program_id(1) == 0)
    def _():
        wb_ref[...] = w_ref[...].astype(BF16)

    o_ref[...] = _dot(a_ref[...], wb_ref[...])


def _matmul(a, w, n_cols, tn, name):
    rows, k = a.shape
    return pl.pallas_call(
        _mm_kernel,
        grid=(n_cols // tn, rows // TMM),
        in_specs=[pl.BlockSpec((TMM, k), lambda j, i: (i, 0)),
                  pl.BlockSpec((k, tn), lambda j, i: (0, j))],
        out_specs=pl.BlockSpec((TMM, tn), lambda j, i: (i, j)),
        out_shape=jax.ShapeDtypeStruct((rows, n_cols), F32),
        scratch_shapes=[pltpu.VMEM((k, tn), BF16)],
        compiler_params=_params("arbitrary", "arbitrary"),
        name=name,
    )(a, w)


def _mm_nt_kernel(a_ref, w_ref, o_ref, wb_ref):
    @pl.when(pl.program_id(1) == 0)
    def _():
        wb_ref[...] = w_ref[...].astype(BF16)

    o_ref[...] = lax.dot_general(a_ref[...], wb_ref[...], NT_DIMS, preferred_element_type=F32)


def _matmul_nt(a, wt, n_cols, tn, name):
    rows, k = a.shape
    return pl.pallas_call(
        _mm_nt_kernel,
        grid=(n_cols // tn, rows // TMM),
        in_specs=[pl.BlockSpec((TMM, k), lambda j, i: (i, 0)),
                  pl.BlockSpec((tn, k), lambda j, i: (j, 0))],
        out_specs=pl.BlockSpec((TMM, tn), lambda j, i: (i, j)),
        out_shape=jax.ShapeDtypeStruct((rows, n_cols), F32),
        scratch_shapes=[pltpu.VMEM((tn, k), BF16)],
        compiler_params=_params("arbitrary", "arbitrary"),
        name=name,
    )(a, wt)


def _mm_nt_shift_kernel(shift, a_ref, w0_ref, w1_ref, o_ref, wb_ref):
    @pl.when(pl.program_id(1) == 0)
    def _():
        w = jnp.concatenate([w0_ref[shift:, :], w1_ref[...]], axis=0)
        wb_ref[...] = w.astype(BF16)

    o_ref[...] = lax.dot_general(a_ref[...], wb_ref[...], NT_DIMS, preferred_element_type=F32)


def _matmul_nt_shifted(a, wt, row0, shift, nt, rot, tn, name):
    rows, k = a.shape
    src = lambda j: lax.rem(j + rot, nt)
    return pl.pallas_call(
        functools.partial(_mm_nt_shift_kernel, shift),
        grid=(nt, rows // TMM),
        in_specs=[pl.BlockSpec((TMM, k), lambda j, i: (i, 0)),
                  pl.BlockSpec((tn, k), lambda j, i: (row0 // tn + src(j), 0)),
                  pl.BlockSpec((shift, k), lambda j, i: ((row0 + (src(j) + 1) * tn) // shift, 0))],
        out_specs=pl.BlockSpec((TMM, tn), lambda j, i: (i, j)),
        out_shape=jax.ShapeDtypeStruct((rows, nt * tn), F32),
        scratch_shapes=[pltpu.VMEM((tn, k), BF16)],
        compiler_params=_params("arbitrary", "arbitrary"),
        name=name,
    )(a, wt, wt)


def _delta_prep(get_qkv, ab, alog, dtb, valid, chunk, seq_block, put_m, l2_ref, vb_ref, kbd_ref, qd_ref, cd_ref):
    rows = ab.shape[0]
    a = ab[:, :128]
    b = ab[:, 128:]
    beta = _sigmoid(b)
    g = -jnp.exp(alog) * _softplus(a + dtb)
    if valid is not None:
        beta = jnp.where(valid, beta, 0.0)
        g = jnp.where(valid, g, 0.0)
    shift = seq_block.bit_length() - 1
    r = lax.broadcasted_iota(jnp.int32, (rows, rows), 0)
    c = lax.broadcasted_iota(jnp.int32, (rows, rows), 1)
    same = (r >> shift) == (c >> shift)
    ltri = jnp.where(same & (r >= c), 1.0, 0.0)
    lall = jnp.where(same, 1.0, 0.0)
    dcol = jnp.dot(ltri, g, preferred_element_type=F32, precision=lax.Precision.HIGHEST)
    last = jnp.dot(lall, g, preferred_element_type=F32, precision=lax.Precision.HIGHEST)
    cd_ref[...] = jnp.exp(last)
    drow = dcol.T
    edc = jnp.exp(dcol)
    etl = jnp.exp(last - dcol)
    rc = lax.broadcasted_iota(jnp.int32, (chunk, chunk), 0)
    cc = lax.broadcasted_iota(jnp.int32, (chunk, chunk), 1)
    same_c = (rc >> shift) == (cc >> shift)
    causal = same_c & (rc >= cc)
    strict = same_c & (rc > cc)
    for h in range(NH):
        cols = slice(h * DH, (h + 1) * DH)
        q, k, v = get_qkv(h)
        qn = q * lax.rsqrt(jnp.sum(q * q, axis=-1, keepdims=True) + EPS) * (DH ** -0.5)
        kn = k * lax.rsqrt(jnp.sum(k * k, axis=-1, keepdims=True) + EPS)
        bcol = beta[:, h:h + 1]
        kb = kn * bcol
        vb_ref[:, cols] = (v * bcol).astype(BF16)
        kbd_ref[:, cols] = (kb * edc[:, h:h + 1]).astype(BF16)
        qd_ref[:, cols] = (qn * edc[:, h:h + 1]).astype(BF16)
        kt = kn * etl[:, h:h + 1]
        for sc in range(rows // chunk):
            rs = slice(sc * chunk, (sc + 1) * chunk)
            diff = dcol[rs, h:h + 1] - drow[h:h + 1, rs]
            lm = jnp.where(causal, jnp.exp(jnp.where(causal, diff, 0.0)), 0.0)
            lhs = jnp.concatenate([kb[rs], qn[rs]], axis=0).astype(BF16)
            s = lax.dot_general(lhs, kn[rs].astype(BF16), NT_DIMS, preferred_element_type=F32)
            idx = sc * NH + h
            put_m(idx, jnp.where(strict, s[:chunk] * lm, 0.0))
            l2_ref[idx] = jnp.concatenate([s[chunk:] * lm, kt[rs].T], axis=0).astype(BF16)


LANES = 128
CPB = TM // CHUNK * NH
assert 2 * CPB == LANES and TPB % 2 == 0


def _solve_lanes(mt_ref, xt_ref):
    sub = lax.broadcasted_iota(jnp.int32, (8, LANES), 0)
    zero = jnp.zeros((8, LANES), F32)
    for i in range(CHUNK):
        groups = i // 8 + 1
        acc = [jnp.where(sub + 8 * v == i, 1.0, 0.0) for v in range(groups)]
        for j in range(i):
            m = mt_ref[i, j:j + 1, :]
            for v in range(j // 8 + 1):
                acc[v] = acc[v] - m * xt_ref[j, 8 * v:8 * v + 8, :]
        for v in range(CHUNK // 8):
            xt_ref[i, 8 * v:8 * v + 8, :] = acc[v] if v < groups else zero


def _prep_prompt_kernel(cur_ref, prev_ref, ab_ref, cw_ref, alog_ref, dtb_ref,
                        t_ref, l2_ref, vb_ref, kbd_ref, qd_ref, cd_ref, xc_ref, mall_ref, mt_ref, xt_ref):
    blk = pl.program_id(1)
    odd = blk % 2
    xc_ref[0:8, :] = jnp.where(blk > 0, prev_ref[...], 0.0)
    xc_ref[8:8 + TM, :] = cur_ref[...]
    w = cw_ref[...]

    def conv(cols):
        y = xc_ref[8:8 + TM, cols] * w[CONVW - 1:CONVW, cols]
        for j in range(CONVW - 1):
            lo = 8 - (CONVW - 1) + j
            y = y + xc_ref[lo:lo + TM, cols] * w[j:j + 1, cols]
        return _silu(y)

    def get_qkv(h):
        return tuple(conv(slice(p * QKW + h * DH, p * QKW + (h + 1) * DH)) for p in range(3))

    def put_m(idx, m):
        start = pl.multiple_of((odd * CPB + idx) * CHUNK, CHUNK)
        mall_ref[pl.ds(start, CHUNK), :] = m

    _delta_prep(get_qkv, ab_ref[...], alog_ref[...], dtb_ref[...], None, CHUNK, CHUNK,
                put_m, l2_ref, vb_ref, kbd_ref, qd_ref, cd_ref)

    @pl.when(odd == 1)
    def _():
        for i in range(CHUNK):
            mt_ref[i] = mall_ref[pl.ds(i, LANES, stride=CHUNK), :].T
        _solve_lanes(mt_ref, xt_ref)
        for i in range(CHUNK):
            mall_ref[pl.ds(i, LANES, stride=CHUNK), :] = xt_ref[i].T
        t_ref[...] = mall_ref[...].astype(BF16)


def _prep_prompt(proj1, proj_ab, conv_w, alog, dtb):
    nch = RP // CHUNK * NH
    row = lambda b, k: b * TPB + k
    tok = pl.BlockSpec((TM, QKW), lambda b, k: (row(b, k), 0))
    t, *rest = pl.pallas_call(
        _prep_prompt_kernel,
        grid=(BP, TPB),
        in_specs=[pl.BlockSpec((TM, CONVC), lambda b, k: (row(b, k), 0)),
                  pl.BlockSpec((8, CONVC), lambda b, k: (jnp.maximum(row(b, k) * (TM // 8) - 1, 0), 0)),
                  pl.BlockSpec((TM, 256), lambda b, k: (row(b, k), 0)),
                  pl.BlockSpec((CONVW, CONVC), lambda b, k: (0, 0)),
                  pl.BlockSpec((1, 128), lambda b, k: (0, 0)),
                  pl.BlockSpec((1, 128), lambda b, k: (0, 0))],
        out_specs=[pl.BlockSpec((LANES * CHUNK, CHUNK), lambda b, k: (row(b, k) // 2, 0)),
                   pl.BlockSpec((CPB, CHUNK + DH, CHUNK), lambda b, k: (row(b, k), 0, 0)),
                   tok, tok, tok,
                   pl.BlockSpec((TM, 128), lambda b, k: (row(b, k), 0))],
        out_shape=[jax.ShapeDtypeStruct((nch * CHUNK, CHUNK), BF16),
                   jax.ShapeDtypeStruct((nch, CHUNK + DH, CHUNK), BF16),
                   jax.ShapeDtypeStruct((RP, QKW), BF16),
                   jax.ShapeDtypeStruct((RP, QKW), BF16),
                   jax.ShapeDtypeStruct((RP, QKW), BF16),
                   jax.ShapeDtypeStruct((RP, 128), F32)],
        scratch_shapes=[pltpu.VMEM((8 + TM, CONVC), F32),
                        pltpu.VMEM((LANES * CHUNK, CHUNK), F32),
                        pltpu.VMEM((CHUNK, CHUNK, LANES), F32),
                        pltpu.VMEM((CHUNK, CHUNK, LANES), F32)],
        compiler_params=_params("arbitrary", "arbitrary"),
        name="delta_prep_prompt",
    )(proj1, proj1, proj_ab, conv_w, alog, dtb)
    return (t.reshape(nch, CHUNK, CHUNK), *rest)


def _prep_sample_kernel(xp_ref, ab_ref, cw_ref, alog_ref, dtb_ref,
                        m_ref, l2_ref, vb_ref, kbd_ref, qd_ref, cd_ref):
    nrow = SSEQ * SPAD
    w = cw_ref[...]
    valid = (lax.broadcasted_iota(jnp.int32, (nrow, 1), 0) & (SPAD - 1)) < TS

    def conv(cols):
        x = xp_ref[:, cols]
        y = x * w[0:1, cols]
        for j in range(1, CONVW):
            y = y + pltpu.roll(x, nrow - j, axis=0) * w[j:j + 1, cols]
        return jnp.where(valid, _silu(y), 0.0)

    def get_qkv(h):
        return tuple(conv(slice(p * QKW + h * DH, p * QKW + (h + 1) * DH)) for p in range(3))

    def put_m(idx, m):
        m_ref[idx] = m

    _delta_prep(get_qkv, ab_ref[...], alog_ref[...], dtb_ref[...], valid, nrow, SPAD,
                put_m, l2_ref, vb_ref, kbd_ref, qd_ref, cd_ref)


def _prep_sample(xp8, ab8, conv_w, alog, dtb):
    nrow = SSEQ * SPAD
    steps = BS // SSEQ
    tok = pl.BlockSpec((nrow, QKW), lambda s: (s, 0))
    mat = pl.BlockSpec((NH, nrow, nrow), lambda s: (s, 0, 0))
    return pl.pallas_call(
        _prep_sample_kernel,
        grid=(steps,),
        in_specs=[pl.BlockSpec((nrow, CONVC), lambda s: (s, 0)),
                  pl.BlockSpec((nrow, 256), lambda s: (s, 0)),
                  pl.BlockSpec((CONVW, CONVC), lambda s: (0, 0)),
                  pl.BlockSpec((1, 128), lambda s: (0, 0)),
                  pl.BlockSpec((1, 128), lambda s: (0, 0))],
        out_specs=[mat, pl.BlockSpec((NH, nrow + DH, nrow), lambda s: (s, 0, 0)),
                   tok, tok, tok, pl.BlockSpec((nrow, 128), lambda s: (s, 0))],
        out_shape=[jax.ShapeDtypeStruct((steps * NH, nrow, nrow), F32),
                   jax.ShapeDtypeStruct((steps * NH, nrow + DH, nrow), BF16),
                   jax.ShapeDtypeStruct((BS * SPAD, QKW), BF16),
                   jax.ShapeDtypeStruct((BS * SPAD, QKW), BF16),
                   jax.ShapeDtypeStruct((BS * SPAD, QKW), BF16),
                   jax.ShapeDtypeStruct((BS * SPAD, 128), F32)],
        compiler_params=_params("arbitrary"),
        name="delta_prep_sample",
    )(xp8, ab8, conv_w, alog, dtb)


def _solve_kernel(m_ref, x_ref):
    n, _, lanes = m_ref.shape
    rowid = lax.broadcasted_iota(jnp.int32, (n, lanes), 0)

    def body_i(i, carry):
        def body_j(j, acc):
            return acc - m_ref[i, pl.ds(j, 1), :] * x_ref[j]

        x_ref[i] = lax.fori_loop(0, i, body_j, (rowid == i).astype(F32))
        return carry

    lax.fori_loop(0, n, body_i, 0)


def _solve(m, name):
    n, _, g = m.shape
    lanes = 128
    spec = pl.BlockSpec((n, n, lanes), lambda s: (0, 0, s))
    return pl.pallas_call(
        _solve_kernel,
        grid=(g // lanes,),
        in_specs=[spec],
        out_specs=spec,
        out_shape=jax.ShapeDtypeStruct((n, n, g), F32),
        compiler_params=_params("arbitrary"),
        name=name,
    )(m)


CPS = 2


def _scan_prompt_kernel(t_ref, l2_ref, vb_ref, kbd_ref, qd_ref, cd_ref, o_ref, sout_ref, s_ref, ub_ref, wd_ref):
    step = pl.program_id(0)

    @pl.when(step == 0)
    def _():
        s_ref[...] = jnp.zeros_like(s_ref)

    chains = [(b, h) for b in range(BP) for h in range(NH)]
    for c in range(CPS):
        rows = slice(c * CHUNK, (c + 1) * CHUNK)
        for n, (b, h) in enumerate(chains):
            cols = slice(h * DH, (h + 1) * DH)
            rhs = jnp.concatenate([vb_ref[b, rows, cols], kbd_ref[b, rows, cols]], axis=1)
            r1 = _dot(t_ref[b, c * NH + h], rhs)
            ub_ref[c * len(chains) + n] = r1[:, :DH]
            wd_ref[c * len(chains) + n] = r1[:, DH:].astype(BF16)
    for c in range(CPS):
        rows = slice(c * CHUNK, (c + 1) * CHUNK)
        r2s = []
        for n, (b, h) in enumerate(chains):
            cols = slice(h * DH, (h + 1) * DH)
            lhs = jnp.concatenate([wd_ref[c * len(chains) + n], qd_ref[b, rows, cols]], axis=0)
            r2s.append(_dot(lhs, s_ref[n].astype(BF16)))
        for n, (b, h) in enumerate(chains):
            cols = slice(h * DH, (h + 1) * DH)
            r2 = r2s[n]
            u = (ub_ref[c * len(chains) + n] - r2[:CHUNK]).astype(BF16)
            r3 = _dot(l2_ref[b, c * NH + h], u)
            o_ref[b, rows, cols] = r2[CHUNK:] + r3[:CHUNK]
            cd = cd_ref[b, c * CHUNK:c * CHUNK + 1, h:h + 1]
            s_ref[n] = s_ref[n] * cd + r3[CHUNK:]

    @pl.when(step == pl.num_programs(0) - 1)
    def _():
        sout_ref[...] = s_ref[...]


def _scan_prompt(t, l2, vb, kbd, qd, cd):
    nchunk = TP // CHUNK
    rows = CPS * CHUNK
    t = t.reshape(BP, nchunk * NH, CHUNK, CHUNK)
    l2 = l2.reshape(BP, nchunk * NH, CHUNK + DH, CHUNK)
    tok3 = lambda a: a.reshape(BP, TP, a.shape[-1])
    tok = pl.BlockSpec((BP, rows, QKW), lambda s: (0, s, 0))
    o, s_fin = pl.pallas_call(
        _scan_prompt_kernel,
        grid=(nchunk // CPS,),
        in_specs=[pl.BlockSpec((BP, CPS * NH, CHUNK, CHUNK), lambda s: (0, s, 0, 0)),
                  pl.BlockSpec((BP, CPS * NH, CHUNK + DH, CHUNK), lambda s: (0, s, 0, 0)),
                  tok, tok, tok,
                  pl.BlockSpec((BP, rows, 128), lambda s: (0, s, 0))],
        out_specs=[tok, pl.BlockSpec((BP * NH, DH, DH), lambda s: (0, 0, 0))],
        out_shape=[jax.ShapeDtypeStruct((BP, TP, QKW), F32),
                   jax.ShapeDtypeStruct((BP * NH, DH, DH), F32)],
        scratch_shapes=[pltpu.VMEM((BP * NH, DH, DH), F32),
                        pltpu.VMEM((CPS * BP * NH, CHUNK, DH), F32),
                        pltpu.VMEM((CPS * BP * NH, CHUNK, DH), BF16)],
        compiler_params=_params("arbitrary"),
        name="delta_scan_prompt",
    )(t, l2, tok3(vb), tok3(kbd), tok3(qd), tok3(cd))
    return o.reshape(RP, QKW), s_fin.reshape(BP, NH, DH, DH)


def _scan_sample_kernel(t_ref, l2_ref, vb_ref, kbd_ref, qd_ref, cd_ref, s0_ref, o_ref, sout_ref):
    nrow = SSEQ * SPAD
    seq_of_row = lax.broadcasted_iota(jnp.int32, (nrow, 1), 0) >> (SPAD.bit_length() - 1)
    for h in range(NH):
        cols = slice(h * DH, (h + 1) * DH)
        rhs = jnp.concatenate([vb_ref[:, cols], kbd_ref[:, cols]], axis=1)
        r1 = _dot(t_ref[h], rhs)
        qd = qd_ref[:, cols].astype(F32)
        u_parts, qs_parts = [], []
        for b in range(SSEQ):
            rows = slice(b * SPAD, (b + 1) * SPAD)
            lhs = jnp.concatenate([r1[rows, DH:], qd[rows]], axis=0).astype(BF16)
            r2 = _dot(lhs, s0_ref[b, h].astype(BF16))
            u_parts.append(r1[rows, :DH] - r2[:SPAD])
            qs_parts.append(r2[SPAD:])
        u = jnp.concatenate(u_parts, axis=0)
        o_ref[:, cols] = jnp.concatenate(qs_parts, axis=0) + _dot(l2_ref[h, :nrow, :], u.astype(BF16))
        ktt = l2_ref[h, nrow:, :]
        for b in range(SSEQ):
            ub = jnp.where(seq_of_row == b, u, 0.0).astype(BF16)
            cd = cd_ref[b * SPAD:b * SPAD + 1, h:h + 1]
            sout_ref[b, h] = s0_ref[b, h] * cd + _dot(ktt, ub)


def _scan_sample(t, l2, vb, kbd, qd, cd, s0):
    nrow = SSEQ * SPAD
    steps = BS // SSEQ
    tok = pl.BlockSpec((nrow, QKW), lambda s: (s, 0))
    st = pl.BlockSpec((SSEQ, NH, DH, DH), lambda s: (s, 0, 0, 0))
    return pl.pallas_call(
        _scan_sample_kernel,
        grid=(steps,),
        in_specs=[pl.BlockSpec((NH, nrow, nrow), lambda s: (s, 0, 0)),
                  pl.BlockSpec((NH, nrow + DH, nrow), lambda s: (s, 0, 0)),
                  tok, tok, tok, pl.BlockSpec((nrow, 128), lambda s: (s, 0)), st],
        out_specs=[tok, st],
        out_shape=[jax.ShapeDtypeStruct((BS * SPAD, QKW), F32),
                   jax.ShapeDtypeStruct((BS, NH, DH, DH), F32)],
        compiler_params=_params("arbitrary"),
        name="delta_scan_sample",
    )(t, l2, vb, kbd, qd, cd, s0)


TO = 256
NPO = RP // TO
TPO = TP // TO
POOL_PREV = 16


def _resident(shape):
    return pl.BlockSpec(shape, lambda i: (0,) * len(shape), pipeline_mode=pl.Buffered(1))


def _branch_a_act(o_ref, z_ref, og):
    parts = []
    for h in range(NH):
        cols = slice(h * DH, (h + 1) * DH)
        parts.append((_rms(o_ref[:, cols], og) * _silu(z_ref[:, cols])).astype(BF16))
    return jnp.concatenate(parts, axis=1)


def _branch_b_proj(yp, pw_ref, ps_ref, wb_ref):
    parts = [_dot(y.astype(BF16), pw_ref[gi]) for gi, y in enumerate(yp)]
    y = jnp.concatenate(parts, axis=1) * ps_ref[...]
    return _dot(y.astype(BF16), wb_ref[...])


def _branch_merge_prompt_kernel(o_ref, z_ref, x_ref, prev_ref, ga_ref, gb_ref,
                                og_ref, wa_ref, pw_ref, ps_ref, wb_ref, m_ref):
    k = pl.program_id(0) % TPO
    ya = _dot(_branch_a_act(o_ref, z_ref, og_ref[...]), wa_ref[...])

    x = x_ref[...]
    v = jnp.concatenate([jnp.where(k > 0, prev_ref[...], 0.0), x], axis=0)
    pos = k * TO + lax.broadcasted_iota(jnp.int32, (TO, 1), 0)
    yp = []
    for gi, win in enumerate(POOL_WINDOWS):
        cols = slice(gi * PGRP, (gi + 1) * PGRP)
        s = v[:, cols]
        shift = 1
        while shift < win:
            s = s + pltpu.roll(s, shift, axis=0)
            shift *= 2
        cnt = jnp.minimum(win, pos + 1).astype(F32)
        yp.append(s[POOL_PREV:] / cnt - x[:, cols])
    yb = _branch_b_proj(yp, pw_ref, ps_ref, wb_ref)

    m_ref[...] = (_sigmoid(ga_ref[...]) * ya + _sigmoid(gb_ref[...]) * yb).astype(BF16)


def _branch_merge_sample_kernel(o_ref, z_ref, x_ref, hist_ref, ga_ref, gb_ref,
                                og_ref, wa_ref, pw_ref, ps_ref, wb_ref, m_ref):
    ya =_dot(_branch_a_act(o_ref, z_ref, og_ref[...]), wa_ref[...])

    def slab(src, cols):
        if src >= PHIST:
            return x_ref[(src - PHIST) * BS:(src - PHIST + 1) * BS, cols]
        return hist_ref[src * BS:(src + 1) * BS, cols]

    yp = []
    for gi, win in enumerate(POOL_WINDOWS):
        cols = slice(gi * PGRP, (gi + 1) * PGRP)
        slabs = []
        for t in range(TS):
            acc = slab(PHIST + t, cols)
            for d in range(1, win):
                acc = acc + slab(PHIST + t - d, cols)
            slabs.append(acc / float(win) - slab(PHIST + t, cols))
        yp.append(jnp.concatenate(slabs, axis=0))
    yb = _branch_b_proj(yp, pw_ref, ps_ref, wb_ref)

    m_ref[...] = (_sigmoid(ga_ref[...]) * ya + _sigmoid(gb_ref[...]) * yb).astype(BF16)


def _branch_merge(o_p, o_s, proj1, proj3, hist_s, o_norm_g, w_proj_a, pool_w, pool_scale, w_proj_b):
    assert all(w & (w - 1) == 0 and w <= POOL_PREV for w in POOL_WINDOWS)
    xcol = 2 * D // PW
    weights = [_resident((1, DH)), _resident((QKW, D)), _resident((len(POOL_WINDOWS), PGRP, PGRP)),
               _resident((1, PW)), _resident((PW, D))]
    wargs = (o_norm_g, w_proj_a, pool_w, pool_scale, w_proj_b)
    merged_p = pl.pallas_call(
        _branch_merge_prompt_kernel,
        grid=(NPO,),
        in_specs=[pl.BlockSpec((TO, QKW), lambda i: (i, 0)),
                  pl.BlockSpec((TO, QKW), lambda i: (i, 3)),
                  pl.BlockSpec((TO, PW), lambda i: (i, xcol)),
                  pl.BlockSpec((POOL_PREV, PW), lambda i: (jnp.maximum(i * (TO // POOL_PREV) - 1, 0), xcol)),
                  pl.BlockSpec((TO, D), lambda i: (i, 0)),
                  pl.BlockSpec((TO, D), lambda i: (i, 1))] + weights,
        out_specs=pl.BlockSpec((TO, D), lambda i: (i, 0)),
        out_shape=jax.ShapeDtypeStruct((RP, D), BF16),
        compiler_params=_params("arbitrary"),
        name="branch_merge_prompt",
    )(o_p, proj1, proj3, proj3, proj3, proj3, *wargs)
    sblk = RP // RS
    merged_s = pl.pallas_call(
        _branch_merge_sample_kernel,
        grid=(1,),
        in_specs=[pl.BlockSpec((RS, QKW), lambda i: (0, 0)),
                  pl.BlockSpec((RS, QKW), lambda i: (sblk, 3)),
                  pl.BlockSpec((RS, PW), lambda i: (sblk, xcol)),
                  pl.BlockSpec((PHIST * BS, PW), lambda i: (0, 0)),
                  pl.BlockSpec((RS, D), lambda i: (sblk, 0)),
                  pl.BlockSpec((RS, D), lambda i: (sblk, 1))] + weights,
        out_specs=pl.BlockSpec((RS, D), lambda i: (0, 0)),
        out_shape=jax.ShapeDtypeStruct((RS, D), BF16),
        compiler_params=_params("arbitrary"),
        name="branch_merge_sample",
    )(o_s, proj1, proj3, hist_s, proj3, proj3, *wargs)
    return merged_p, merged_s


NCH = 512


def _proj_residual(a_ref, w_ref, x_ref, gate, out_ref):
    ss = 0.0
    for n in range(D // NCH):
        cols = slice(n * NCH, (n + 1) * NCH)
        r = x_ref[:, cols] + gate[:, cols] * _dot(a_ref[...], w_ref[:, cols])
        out_ref[:, cols] = r
        ss = ss + jnp.sum(r * r, axis=-1, keepdims=True)
    return ss


def _tile_rows(v, rows):
    return jnp.concatenate([v] * (rows // BS), axis=0)


def _post_out_kernel(mp_ref, ms_ref, xp_ref, xs_ref,
                     gtp_ref, shp_ref, scp_ref, gts_ref, shs_ref, scs_ref, g2_ref, w_ref,
                     x1_ref, h_ref):
    i = pl.program_id(0)

    def body(m_ref, x_ref, gt, sh, sc):
        ss = _proj_residual(m_ref, w_ref, x_ref, gt, x1_ref)
        inv = lax.rsqrt(ss * (1.0 / D) + EPS)
        h_ref[...] = (x1_ref[...] * inv * g2_ref[...] * (1.0 + sc) + sh).astype(BF16)

    @pl.when(i < NPT)
    def _():
        body(mp_ref, xp_ref, gtp_ref[0], shp_ref[0], scp_ref[0])

    @pl.when(i == NPT)
    def _():
        body(ms_ref, xs_ref, _tile_rows(gts_ref[...], TM), _tile_rows(shs_ref[...], TM), _tile_rows(scs_ref[...], TM))


def _post_out(merged_p, merged_s, xp2, xs2, mod_p, mod_s, norm2_g, w_out):
    assert RS == TM
    gtp, gts = _mod_specs(2, TPB)
    shp, shs = _mod_specs(3, TPB)
    scp, scs = _mod_specs(4, TPB)
    tile = pl.BlockSpec((TM, D), lambda i: (i, 0))
    ptile = pl.BlockSpec((TM, D), lambda i: (jnp.minimum(i, NPT - 1), 0))
    return pl.pallas_call(
        _post_out_kernel,
        grid=(NPT + 1,),
        in_specs=[ptile, _resident((TM, D)),
                  ptile, _resident((TM, D)),
                  gtp, shp, scp, gts, shs, scs,
                  _resident((1, D)),
                  _resident((D, D))],
        out_specs=[tile, tile],
        out_shape=[jax.ShapeDtypeStruct((R, D), F32), jax.ShapeDtypeStruct((R, D), BF16)],
        compiler_params=_params("arbitrary"),
        name="out_proj_norm2",
    )(merged_p, merged_s, xp2, xs2, mod_p, mod_p, mod_p, mod_s, mod_s, mod_s, norm2_g, w_out)


FF_TN = 512
FF_TM = R // 4


def _ffn1_kernel(h_ref, wg_ref, wu_ref, a_ref, wgb_ref, wub_ref):
    @pl.when(pl.program_id(1) == 0)
    def _():
        wgb_ref[...] = wg_ref[...].astype(BF16)
        wub_ref[...] = wu_ref[...].astype(BF16)

    h = h_ref[...]
    a_ref[...] = (_silu(_dot(h, wgb_ref[...])) * _dot(h, wub_ref[...])).astype(BF16)


def _ffn1(h, w_gate_up):
    nj = DFF // FF_TN
    return pl.pallas_call(
        _ffn1_kernel,
        grid=(nj, R // FF_TM),
        in_specs=[pl.BlockSpec((FF_TM, D), lambda j, i: (i, 0)),
                  pl.BlockSpec((D, FF_TN), lambda j, i: (0, j)),
                  pl.BlockSpec((D, FF_TN), lambda j, i: (0, nj + j))],
        out_specs=pl.BlockSpec((FF_TM, FF_TN), lambda j, i: (i, j)),
        out_shape=jax.ShapeDtypeStruct((R, DFF), BF16),
        scratch_shapes=[pltpu.VMEM((D, FF_TN), BF16), pltpu.VMEM((D, FF_TN), BF16)],
        compiler_params=_params("arbitrary", "arbitrary"),
        name="ffn_gate_up",
    )(h, w_gate_up, w_gate_up)


def _ffn2_kernel(a_ref, w_ref, x1_ref, gtp_ref, gts_ref, fg_ref, yp_ref, ys_ref):
    i = pl.program_id(0)

    def body(gt, y_ref):
        ss = _proj_residual(a_ref, w_ref, x1_ref, gt, y_ref)
        y_ref[...] = y_ref[...] * lax.rsqrt(ss * (1.0 / D) + EPS) * fg_ref[...]

    @pl.when(i < NPO)
    def _():
        body(gtp_ref[0], yp_ref)

    @pl.when(i >= NPO)
    def _():
        body(_tile_rows(gts_ref[...], TO), ys_ref)


def _ffn2(act, w_down, x1, mod_p, mod_s, final_g):
    gtp, gts = _mod_specs(5, TPO)
    return pl.pallas_call(
        _ffn2_kernel,
        grid=(R // TO,),
        in_specs=[pl.BlockSpec((TO, DFF), lambda i: (i, 0)),
                  _resident((DFF, D)),
                  pl.BlockSpec((TO, D), lambda i: (i, 0)),
                  gtp, gts,
                  _resident((1, D))],
        out_specs=[pl.BlockSpec((TO, D), lambda i: (jnp.minimum(i, NPO - 1), 0)),
                   pl.BlockSpec((TO, D), lambda i: (jnp.maximum(i - NPO, 0), 0))],
        out_shape=[jax.ShapeDtypeStruct((RP, D), F32), jax.ShapeDtypeStruct((RS, D), F32)],
        compiler_params=_params("arbitrary"),
        name="ffn_down_final",
    )(act, w_down, x1, mod_p, mod_s, final_g)


def _to_time_major(x):
    return jnp.transpose(x, (1, 0, 2)).reshape(TS * BS, x.shape[-1])


def _to_batch_major(x):
    return jnp.transpose(x.reshape(TS, BS, x.shape[-1]), (1, 0, 2))


def _pad_lanes(v):
    return jnp.zeros((1, 128), F32).at[0, :NH].set(v.astype(F32))


def kernel(x_prompt, x_sample, c_prompt, c_sample, state_delta, state_conv, state_pool, w_ada, b_ada, norm1_g,
           w_in, conv_w, a_log, dt_bias, o_norm_g, pool_w, pool_scale, w_proj_a, w_proj_b, w_out, norm2_g,
           w_gate_up, w_down, final_g):
    assert w_ada.shape[0] == 1, "single layer"
    xp2 = x_prompt.reshape(RP, D)
    xs2 = _to_time_major(x_sample)

    c_all = jnp.concatenate([c_sample, c_prompt, jnp.zeros((4, D), F32)], axis=0)
    mod = _mod(c_all, w_ada[0], b_ada[0].reshape(1, 6 * D))
    mod_s = mod[:BS]
    mod_p = mod[BS:BS + BP].reshape(BP, 1, 6 * D)

    u = _pre(xp2, xs2, mod_p, mod_s, norm1_g[0].reshape(1, D))

    w_in0 = w_in[0]
    o_ab = 4 * QKW
    w_ab = jnp.zeros((D, 256), F32)
    w_ab = w_ab.at[:, :NH].set(w_in0[:, o_ab:o_ab + NH]).at[:, 128:128 + NH].set(w_in0[:, o_ab + NH:o_ab + 2 * NH])
    w_in_t = jnp.transpose(w_in0)
    proj1 = _matmul_nt(u, w_in_t, 4 * QKW, 1024, "in_proj_qkvz")
    proj_ab = _matmul(u, w_ab, 256, 256, "in_proj_ab")
    proj3 = _matmul_nt_shifted(u, w_in_t, o_ab, 2 * NH, (PW + 2 * D) // 1024, PW // 1024, 1024, "in_proj_gates_pool")

    cw = conv_w[0]
    alog = _pad_lanes(a_log[0])
    dtb = _pad_lanes(dt_bias[0])

    t_p, l2_p, vb_p, kbd_p, qd_p, cd_p = _prep_prompt(proj1, proj_ab, cw, alog, dtb)
    o_p, s_p = _scan_prompt(t_p, l2_p, vb_p, kbd_p, qd_p, cd_p)

    qkv_s = _to_batch_major(proj1[RP:, :CONVC])
    conv_in = jnp.concatenate([state_conv[0], qkv_s, jnp.zeros((BS, SPAD - TS - (CONVW - 1), CONVC), F32)], axis=1)
    ab_s = _to_batch_major(proj_ab[RP:])
    ab_s = jnp.concatenate([ab_s, jnp.zeros((BS, SPAD - TS, 256), F32)], axis=1)
    m_s, l2_s, vb_s, kbd_s, qd_s, cd_s = _prep_sample(
        conv_in.reshape(BS * SPAD, CONVC), ab_s.reshape(BS * SPAD, 256), cw, alog, dtb)
    nblk = BS // SSEQ
    m_blocks = jnp.diagonal(m_s.reshape(nblk * NH, SSEQ, SPAD, SSEQ, SPAD), axis1=1, axis2=3)
    m_blocks = jnp.transpose(m_blocks, (1, 2, 0, 3)).reshape(SPAD, SPAD, nblk * NH * SSEQ)
    t_blocks = _solve(m_blocks, "tri_solve_sample").reshape(SPAD, SPAD, nblk * NH, SSEQ)
    t_s = jnp.einsum("ijgb,bc->gbicj", t_blocks, jnp.eye(SSEQ, dtype=F32))
    t_s = t_s.reshape(nblk * NH, SSEQ * SPAD, SSEQ * SPAD).astype(BF16)
    o_s8, s_s = _scan_sample(t_s, l2_s, vb_s, kbd_s, qd_s, cd_s, state_delta[0])
    o_s = _to_time_major(o_s8.reshape(BS, SPAD, QKW)[:, :TS])

    hist_s = jnp.transpose(state_pool[0], (1, 0, 2)).reshape(PHIST * BS, PW)
    merged_p, merged_s = _branch_merge(o_p, o_s, proj1, proj3, hist_s, o_norm_g[0].reshape(1, DH),
                                       w_proj_a[0].astype(BF16), pool_w[0].astype(BF16),
                                       pool_scale[0].reshape(1, PW), w_proj_b[0].astype(BF16))
    x1, h = _post_out(merged_p, merged_s, xp2, xs2, mod_p, mod_s, norm2_g[0].reshape(1, D), w_out[0].astype(BF16))
    act = _ffn1(h, w_gate_up[0])
    y_p, y_s = _ffn2(act, w_down[0].astype(BF16), x1, mod_p, mod_s, final_g.reshape(1, D))

    xpool_s = _to_batch_major(proj3[RP:, 2 * D:])

    def tail_rows(a, n, c0, c1):
        return jnp.stack([a[(b + 1) * TP - n:(b + 1) * TP, c0:c1] for b in range(BP)])

    return (y_p.reshape(BP, TP, D),
            _to_batch_major(y_s),
            s_p[None],
            tail_rows(proj1, CONVW - 1, 0, CONVC)[None],
            tail_rows(proj3, PHIST, 2 * D, 2 * D + PW)[None],
            s_s[None],
            conv_in[:, TS:TS + CONVW - 1][None],
            jnp.concatenate([state_pool[0][:, TS:], xpool_s], axis=1)[None])
```
